```python
import jax, jax.numpy as jnp
from jax import lax
import numpy as np

D_MODEL = 4096
BATCH = 4
SEQ = 2048
DEPTH = 4
DEC_BATCH = 128
DEC_SEQ = 4
PAST_LEN = 16384
PAGE_SIZE = 128

N_BRANCH = 4
BRANCH_W = D_MODEL // 4
POOL_GROUPS = 4
POOL_WINDOWS = (2, 4, 8, 16)
POOL_GROUP_W = BRANCH_W // POOL_GROUPS
POOL_CTX = max(POOL_WINDOWS) - 1
CONV_B_WIDTH = 31
CONV_C_WIDTH = 3
N_MEM = 256
N_XHEADS = 4
XHEAD_DIM = BRANCH_W // N_XHEADS
D_FF = 2 * D_MODEL
EPS = 1e-6
OFF_A = 0
OFF_B = OFF_A + BRANCH_W
OFF_C = OFF_B + 2 * BRANCH_W
OFF_Q = OFF_C + 3 * BRANCH_W
OFF_G = OFF_Q + BRANCH_W
D_IN = OFF_G + N_BRANCH * D_MODEL

kernel_name = 'gated_parallel_pool_conv_memory_decoder_step'


def rmsnorm(x, g):
    xf = x.astype(jnp.float32)
    r = lax.rsqrt(jnp.mean(xf * xf, axis=-1, keepdims=True) + EPS)
    return (xf * r).astype(x.dtype) * g


def layernorm(x, g, b):
    xf = x.astype(jnp.float32)
    mu = jnp.mean(xf, axis=-1, keepdims=True)
    var = jnp.mean(jnp.square(xf - mu), axis=-1, keepdims=True)
    return ((xf - mu) * lax.rsqrt(var + EPS)).astype(x.dtype) * g + b


def swiglu_ffn(h, w_in, w_out):
    a, b = jnp.split(h @ w_in, 2, axis=-1)
    return (jax.nn.silu(a) * b) @ w_out


def multiscale_pool(u_ext, pos):
    bsz, n_ext, _ = u_ext.shape
    L = n_ext - POOL_CTX
    uf = u_ext.astype(jnp.float32).reshape(bsz, n_ext, POOL_GROUPS, POOL_GROUP_W)
    cs = jnp.concatenate([jnp.zeros_like(uf[:, :1]), jnp.cumsum(uf, axis=1)], axis=1)
    hi = cs[:, POOL_CTX + 1:]
    lo = jnp.stack([cs[:, POOL_CTX + 1 - w: POOL_CTX + 1 - w + L, g] for g, w in enumerate(POOL_WINDOWS)], axis=2)
    win = jnp.array(POOL_WINDOWS, jnp.float32)
    cnt = jnp.minimum(win[None, :], (pos[:, None] + 1).astype(jnp.float32))
    out = (hi - lo) / cnt[None, :, :, None] - uf[:, POOL_CTX:]
    return out.reshape(bsz, L, BRANCH_W).astype(u_ext.dtype)


def causal_dwconv(u_ext, w):
    return lax.conv_general_dilated(u_ext, w[:, None, :].astype(u_ext.dtype), window_strides=(1,), padding='VALID',
                                    dimension_numbers=('NWC', 'WIO', 'NWC'), feature_group_count=u_ext.shape[-1])


def mem_kv(mem, g, w_kv):
    bsz = mem.shape[0]
    k, v = jnp.split(rmsnorm(mem, g) @ w_kv, 2, axis=-1)
    return (k.reshape(bsz, N_MEM, N_XHEADS, XHEAD_DIM), v.reshape(bsz, N_MEM, N_XHEADS, XHEAD_DIM))


def mem_attention(q, k, v):
    s = jnp.einsum('blhd,bmhd->bhlm', q, k, preferred_element_type=jnp.float32) * (XHEAD_DIM ** -0.5)
    p = jax.nn.softmax(s, axis=-1).astype(v.dtype)
    o = jnp.einsum('bhlm,bmhd->blhd', p, v)
    return o.reshape(q.shape[0], q.shape[1], BRANCH_W)


def token_mix(h, pool_ctx, convb_ctx, convc_ctx, mem_k, mem_v, pos, w_in, pool_lin, pool_scale,
              conv_b_w, conv_b_bias, ln_b_gain, ln_b_bias, conv_c_w, w_branch, w_o):
    bsz, L, _ = h.shape
    z = h @ w_in
    u_a = z[..., OFF_A:OFF_B]
    b_val = z[..., OFF_B:OFF_B + BRANCH_W]
    b_gate = z[..., OFF_B + BRANCH_W:OFF_C]
    c_h = z[..., OFF_C:OFF_C + BRANCH_W]
    c_bg = z[..., OFF_C + BRANCH_W:OFF_C + 2 * BRANCH_W]
    c_cg = z[..., OFF_C + 2 * BRANCH_W:OFF_Q]
    q = z[..., OFF_Q:OFF_G].reshape(bsz, L, N_XHEADS, XHEAD_DIM)
    gates = jax.nn.sigmoid(z[..., OFF_G:].reshape(bsz, L, N_BRANCH, D_MODEL))
    a_ext = jnp.concatenate([pool_ctx, u_a], axis=1)
    pooled = multiscale_pool(a_ext, pos).reshape(bsz, L, POOL_GROUPS, POOL_GROUP_W)
    br_a = jnp.einsum('blgc,gcd->blgd', pooled, pool_lin).reshape(bsz, L, BRANCH_W) * pool_scale
    glu = b_val * jax.nn.sigmoid(b_gate)
    b_ext = jnp.concatenate([convb_ctx, glu], axis=1)
    br_b = jax.nn.silu(layernorm(causal_dwconv(b_ext, conv_b_w) + conv_b_bias, ln_b_gain, ln_b_bias))
    c_ext = jnp.concatenate([convc_ctx, c_cg * c_h], axis=1)
    br_c = c_bg * causal_dwconv(c_ext, conv_c_w)
    br_x = mem_attention(q, mem_k, mem_v)
    branches = (br_a, br_b, br_c, br_x)
    merged = gates[:, :, 0] * (branches[0] @ w_branch[0])
    for n in range(1, N_BRANCH):
        merged = merged + gates[:, :, n] * (branches[n] @ w_branch[n])
    return (merged @ w_o, a_ext[:, -POOL_CTX:], b_ext[:, -(CONV_B_WIDTH - 1):], c_ext[:, -(CONV_C_WIDTH - 1):])


def block(x, pool_ctx, convb_ctx, convc_ctx, mem_k, mem_v, pos, lp):
    (n1, f1_in, f1_out, nmix, w_in, pool_lin, pool_scale, cbw, cbb, lng, lnb, ccw, w_branch, w_o, n2, f2_in, f2_out) = lp
    x = x + 0.5 * swiglu_ffn(rmsnorm(x, n1), f1_in, f1_out)
    mix, new_pool, new_cb, new_cc = token_mix(rmsnorm(x, nmix), pool_ctx, convb_ctx, convc_ctx, mem_k, mem_v, pos,
                                             w_in, pool_lin, pool_scale, cbw, cbb, lng, lnb, ccw, w_branch, w_o)
    x = x + mix
    x = x + 0.5 * swiglu_ffn(rmsnorm(x, n2), f2_in, f2_out)
    return x, new_pool, new_cb, new_cc


def setup_inputs(seed: int = 0) -> dict:
    key = jax.random.key(seed)
    ks = iter(jax.random.split(key, 40))
    nrm = lambda shape, s: jax.random.normal(next(ks), shape, jnp.float32) * s
    gain = lambda shape: 1.0 + nrm(shape, 0.02)
    return {
        'x_prompt': nrm((BATCH, SEQ, D_MODEL), 1.0),
        'x_sample': nrm((DEC_BATCH, DEC_SEQ, D_MODEL), 1.0),
        'cache_mem_k': nrm((DEPTH, DEC_BATCH, N_MEM, N_XHEADS, XHEAD_DIM), 1.0),
        'cache_mem_v': nrm((DEPTH, DEC_BATCH, N_MEM, N_XHEADS, XHEAD_DIM), 1.0),
        'state_pool': nrm((DEPTH, DEC_BATCH, POOL_CTX, BRANCH_W), 1.0),
        'state_conv_b': nrm((DEPTH, DEC_BATCH, CONV_B_WIDTH - 1, BRANCH_W), 0.5),
        'state_conv_c': nrm((DEPTH, DEC_BATCH, CONV_C_WIDTH - 1, BRANCH_W), 0.5),
        'mem_prompt': nrm((BATCH, N_MEM, D_MODEL), 1.0),
        'norm_ff1': gain((DEPTH, D_MODEL)),
        'w_ff1_in': nrm((DEPTH, D_MODEL, 2 * D_FF), D_MODEL ** -0.5),
        'w_ff1_out': nrm((DEPTH, D_FF, D_MODEL), D_FF ** -0.5),
        'norm_mix': gain((DEPTH, D_MODEL)),
        'w_in': nrm((DEPTH, D_MODEL, D_IN), D_MODEL ** -0.5),
        'pool_lin': nrm((DEPTH, POOL_GROUPS, POOL_GROUP_W, POOL_GROUP_W), POOL_GROUP_W ** -0.5),
        'pool_scale': gain((DEPTH, BRANCH_W)),
        'conv_b_w': nrm((DEPTH, CONV_B_WIDTH, BRANCH_W), CONV_B_WIDTH ** -0.5),
        'conv_b_bias': nrm((DEPTH, BRANCH_W), 0.02),
        'ln_b_gain': gain((DEPTH, BRANCH_W)),
        'ln_b_bias': nrm((DEPTH, BRANCH_W), 0.02),
        'conv_c_w': nrm((DEPTH, CONV_C_WIDTH, BRANCH_W), CONV_C_WIDTH ** -0.5),
        'norm_mem': gain((DEPTH, D_MODEL)),
        'w_mem_kv': nrm((DEPTH, D_MODEL, 2 * BRANCH_W), D_MODEL ** -0.5),
        'w_branch': nrm((DEPTH, N_BRANCH, BRANCH_W, D_MODEL), BRANCH_W ** -0.5),
        'w_o': nrm((DEPTH, D_MODEL, D_MODEL), D_MODEL ** -0.5),
        'norm_ff2': gain((DEPTH, D_MODEL)),
        'w_ff2_in': nrm((DEPTH, D_MODEL, 2 * D_FF), D_MODEL ** -0.5),
        'w_ff2_out': nrm((DEPTH, D_FF, D_MODEL), D_FF ** -0.5),
        'norm_final': gain((D_MODEL,)),
    }


def reference(x_prompt, x_sample, cache_mem_k, cache_mem_v, state_pool, state_conv_b, state_conv_c, mem_prompt,
              norm_ff1, w_ff1_in, w_ff1_out, norm_mix, w_in, pool_lin, pool_scale, conv_b_w, conv_b_bias,
              ln_b_gain, ln_b_bias, conv_c_w, norm_mem, w_mem_kv, w_branch, w_o, norm_ff2, w_ff2_in, w_ff2_out,
              norm_final):
    bp, lp_len, _ = x_prompt.shape
    ls_len = x_sample.shape[1]
    pos_p = jnp.arange(lp_len, dtype=jnp.int32)
    pos_s = PAST_LEN + jnp.arange(ls_len, dtype=jnp.int32)
    zp_pool = jnp.zeros((bp, POOL_CTX, BRANCH_W), x_prompt.dtype)
    zp_cb = jnp.zeros((bp, CONV_B_WIDTH - 1, BRANCH_W), x_prompt.dtype)
    zp_cc = jnp.zeros((bp, CONV_C_WIDTH - 1, BRANCH_W), x_prompt.dtype)
    xp, xs = x_prompt, x_sample
    mk_p, mv_p, pool_p, cb_p, cc_p, pool_s, cb_s, cc_s = [], [], [], [], [], [], [], []
    for l in range(DEPTH):
        lp = (norm_ff1[l], w_ff1_in[l], w_ff1_out[l], norm_mix[l], w_in[l], pool_lin[l], pool_scale[l],
              conv_b_w[l], conv_b_bias[l], ln_b_gain[l], ln_b_bias[l], conv_c_w[l], w_branch[l], w_o[l],
              norm_ff2[l], w_ff2_in[l], w_ff2_out[l])
        k_p, v_p = mem_kv(mem_prompt, norm_mem[l], w_mem_kv[l])
        xp, np_pool, np_cb, np_cc = block(xp, zp_pool, zp_cb, zp_cc, k_p, v_p, pos_p, lp)
        mk_p.append(k_p); mv_p.append(v_p); pool_p.append(np_pool); cb_p.append(np_cb); cc_p.append(np_cc)
        xs, ns_pool, ns_cb, ns_cc = block(xs, state_pool[l], state_conv_b[l], state_conv_c[l],
                                          cache_mem_k[l], cache_mem_v[l], pos_s, lp)
        pool_s.append(ns_pool); cb_s.append(ns_cb); cc_s.append(ns_cc)
    y_prompt = rmsnorm(xp, norm_final)
    y_sample = rmsnorm(xs, norm_final)
    return (y_prompt, y_sample, jnp.stack(mk_p), jnp.stack(mv_p), jnp.stack(pool_p), jnp.stack(cb_p), jnp.stack(cc_p),
            jnp.stack(pool_s), jnp.stack(cb_s), jnp.stack(cc_s))
```

```python
import functools

import jax
import jax.numpy as jnp
from jax import lax
from jax.experimental import pallas as pl
from jax.experimental.pallas import tpu as pltpu

EPS = 1e-6
PAST_LEN = 16384
POOL_WINDOWS = (2, 4, 8, 16)
N_BRANCH = 4

BF16 = jnp.bfloat16
F32 = jnp.float32

_V7X_VMEM_BYTES = 64 * 2**20
_VMEM_CAP = _V7X_VMEM_BYTES - 6 * 2**20
_SUBLANES = 8
_BF16_ROWS = 16
_MXU_COLS = 256


def _params(semantics, vmem_bytes):
    return pltpu.CompilerParams(dimension_semantics=semantics,
                                vmem_limit_bytes=int(min(_VMEM_CAP, vmem_bytes)))


def _largest_divisor(n, limit, multiple):
    for d in range(min(n, limit), 0, -1):
        if n % d == 0 and d % multiple == 0:
            return d
    raise ValueError(f"no block size for {n} (limit {limit}, multiple {multiple})")


def _rmsnorm_kernel(x_ref, g_ref, o_ref):
    x = x_ref[...]
    r = lax.rsqrt(jnp.mean(x * x, axis=-1, keepdims=True) + EPS)
    o_ref[...] = ((x * r) * g_ref[...]).astype(o_ref.dtype)


def _rmsnorm(x, gain, out_dtype):
    m, d = x.shape
    bm = _largest_divisor(m, 512, _BF16_ROWS)
    osz = jnp.dtype(out_dtype).itemsize
    vmem = 2 * bm * d * (4 + osz) + 4 * bm * d * 4
    return pl.pallas_call(
        _rmsnorm_kernel,
        out_shape=jax.ShapeDtypeStruct((m, d), out_dtype),
        grid=(m // bm,),
        in_specs=[pl.BlockSpec((bm, d), lambda i: (i, 0)),
                  pl.BlockSpec((1, d), lambda i: (0, 0))],
        out_specs=pl.BlockSpec((bm, d), lambda i: (i, 0)),
        compiler_params=_params(("parallel",), vmem),
        name="rmsnorm",
    )(x, gain)


def _dot(x, w_ref):
    return jnp.dot(x, w_ref[...].astype(BF16), preferred_element_type=F32)


def _mm_plain_kernel(x_ref, w_ref, o_ref):
    o_ref[...] = _dot(x_ref[...], w_ref).astype(o_ref.dtype)


def _mm_sigmoid_kernel(x_ref, w_ref, o_ref):
    o_ref[...] = jax.nn.sigmoid(_dot(x_ref[...], w_ref)).astype(o_ref.dtype)


def _mm_swiglu_kernel(x_ref, wa_ref, wb_ref, o_ref):
    x = x_ref[...]
    a = _dot(x, wa_ref)
    b = _dot(x, wb_ref)
    o_ref[...] = (jax.nn.silu(a) * b).astype(o_ref.dtype)


def _mm_residual_kernel(x_ref, w_ref, r_ref, o_ref, *, scale):
    o_ref[...] = r_ref[...] + scale * _dot(x_ref[...], w_ref)


def _mm_blocks(m, k, n_weight_blocks):
    bn = _MXU_COLS
    bm = _largest_divisor(m, max(_BF16_ROWS, (18 * 2**20) // (2 * k)), _BF16_ROWS)
    vmem = (bm * k * 2
            + n_weight_blocks * 2 * k * bn * 4
            + n_weight_blocks * k * bn * 2
            + 10 * bm * bn * 4)
    return bm, bn, vmem


def _x_spec(bm, k):
    return pl.BlockSpec((bm, k), lambda i, j: (i, 0), pipeline_mode=pl.Buffered(1))


def _w_spec(k, bn, layer, block0):
    return pl.BlockSpec((None, k, bn), lambda i, j: (layer, 0, block0 + j))


def _matmul(x, w_stack, layer, col0, n, *, out_dtype, sigmoid=False):
    m, k = x.shape
    bm, bn, vmem = _mm_blocks(m, k, 1)
    kern = _mm_sigmoid_kernel if sigmoid else _mm_plain_kernel
    return pl.pallas_call(
        kern,
        out_shape=jax.ShapeDtypeStruct((m, n), out_dtype),
        grid=(m // bm, n // bn),
        in_specs=[_x_spec(bm, k), _w_spec(k, bn, layer, col0 // bn)],
        out_specs=pl.BlockSpec((bm, bn), lambda i, j: (i, j)),
        compiler_params=_params(("parallel", "arbitrary"), vmem),
        name="proj_sigmoid" if sigmoid else "proj",
    )(x, w_stack)


def _matmul_swiglu(x, w_stack, layer):
    m, k = x.shape
    f = w_stack.shape[-1] // 2
    bm, bn, vmem = _mm_blocks(m, k, 2)
    return pl.pallas_call(
        _mm_swiglu_kernel,
        out_shape=jax.ShapeDtypeStruct((m, f), BF16),
        grid=(m // bm, f // bn),
        in_specs=[_x_spec(bm, k), _w_spec(k, bn, layer, 0), _w_spec(k, bn, layer, f // bn)],
        out_specs=pl.BlockSpec((bm, bn), lambda i, j: (i, j)),
        compiler_params=_params(("parallel", "arbitrary"), vmem),
        name="ffn_in_swiglu",
    )(x, w_stack, w_stack)


def _matmul_residual(x, w_stack, layer, res, scale):
    m, k = x.shape
    n = w_stack.shape[-1]
    bm, bn, vmem = _mm_blocks(m, k, 1)
    return pl.pallas_call(
        functools.partial(_mm_residual_kernel, scale=scale),
        out_shape=jax.ShapeDtypeStruct((m, n), F32),
        grid=(m // bm, n // bn),
        in_specs=[_x_spec(bm, k), _w_spec(k, bn, layer, 0),
                  pl.BlockSpec((bm, bn), lambda i, j: (i, j))],
        out_specs=pl.BlockSpec((bm, bn), lambda i, j: (i, j)),
        compiler_params=_params(("parallel", "arbitrary"), vmem),
        name="proj_residual",
    )(x, w_stack, res)


def _merge_kernel(ba_ref, bb_ref, bc_ref, bx_ref, g0_ref, g1_ref, g2_ref, g3_ref,
                  w0_ref, w1_ref, w2_ref, w3_ref, o_ref):
    acc = g0_ref[...].astype(F32) * _dot(ba_ref[...], w0_ref)
    acc = acc + g1_ref[...].astype(F32) * _dot(bb_ref[...], w1_ref)
    acc = acc + g2_ref[...].astype(F32) * _dot(bc_ref[...], w2_ref)
    acc = acc + g3_ref[...].astype(F32) * _dot(bx_ref[...], w3_ref)
    o_ref[...] = acc.astype(o_ref.dtype)


def _merge(branches, gates, w_branch, layer, d):
    m, bw = branches[0].shape
    bn = _MXU_COLS
    bm = _largest_divisor(m, (18 * 2**20) // (2 * N_BRANCH * bw), _BF16_ROWS)
    nb = d // bn
    vmem = (N_BRANCH * bm * bw * 2 + N_BRANCH * 2 * bw * bn * 4 + N_BRANCH * bw * bn * 2
            + N_BRANCH * 2 * bm * bn * 2 + 12 * bm * bn * 4)
    br_spec = pl.BlockSpec((bm, bw), lambda i, j: (i, 0), pipeline_mode=pl.Buffered(1))
    gate_specs = [pl.BlockSpec((bm, bn), functools.partial(lambda i, j, n: (i, n * nb + j), n=n))
                  for n in range(N_BRANCH)]
    w_specs = [pl.BlockSpec((None, None, bw, bn), functools.partial(lambda i, j, n: (layer, n, 0, j), n=n))
               for n in range(N_BRANCH)]
    return pl.pallas_call(
        _merge_kernel,
        out_shape=jax.ShapeDtypeStruct((m, d), BF16),
        grid=(m // bm, nb),
        in_specs=[br_spec] * N_BRANCH + gate_specs + w_specs,
        out_specs=pl.BlockSpec((bm, bn), lambda i, j: (i, j)),
        compiler_params=_params(("parallel", "arbitrary"), vmem),
        name="gated_merge",
    )(*branches, *([gates] * N_BRANCH), *([w_branch] * N_BRANCH))


def _layernorm_silu(y, gain, bias):
    mu = jnp.mean(y, axis=-1, keepdims=True)
    c = y - mu
    var = jnp.mean(c * c, axis=-1, keepdims=True)
    return jax.nn.silu(c * lax.rsqrt(var + EPS) * gain + bias)


def _attend(q, k_of, v_of, n_heads, hd):
    outs = []
    for h in range(n_heads):
        qh = q[:, h * hd:(h + 1) * hd].astype(BF16)
        s = lax.dot_general(qh, k_of(h).astype(BF16), (((1,), (1,)), ((), ())),
                            preferred_element_type=F32) * (hd ** -0.5)
        e = jnp.exp(s - jnp.max(s, axis=-1, keepdims=True))
        p = (e / jnp.sum(e, axis=-1, keepdims=True)).astype(BF16)
        outs.append(jnp.dot(p, v_of(h).astype(BF16), preferred_element_type=F32))
    return outs


_ROW_CHUNK = 64
_LANE_CHUNK = 256


def _prompt_mixer_kernel(ua_ref, bv_ref, bg_ref, ch_ref, cbg_ref, ccg_ref, q_ref, k_ref, v_ref,
                         plin_ref, pscale_ref, cbw_ref, cbb_ref, lng_ref, lnb_ref, ccw_ref,
                         bra_ref, brb_ref, brc_ref, brx_ref, npool_ref, ncb_ref, ncc_ref,
                         exta, extb, extc, conv_scr,
                         *, tile, pad_a, pad_b, pad_c, n_heads):
    t = pl.program_id(1)
    bw = ua_ref.shape[-1]
    gw = bw // len(POOL_WINDOWS)
    hd = bw // n_heads
    kb = cbw_ref.shape[0]
    kc = ccw_ref.shape[0]
    pool_ctx = npool_ref.shape[0]

    @pl.when(t == 0)
    def _():
        exta[pl.ds(0, pad_a), :] = jnp.zeros((pad_a, bw), F32)
        extb[pl.ds(0, pad_b), :] = jnp.zeros((pad_b, bw), F32)
        extc[pl.ds(0, pad_c), :] = jnp.zeros((pad_c, bw), F32)

    ua = ua_ref[...]
    exta[pl.ds(pad_a, tile), :] = ua
    pos = t * tile + lax.broadcasted_iota(jnp.int32, (tile, 1), 0)
    for g, w in enumerate(POOL_WINDOWS):
        lanes = slice(g * gw, (g + 1) * gw)
        s = ua[:, lanes]
        for i in range(1, w):
            s = s + exta[pl.ds(pad_a - i, tile), lanes]
        cnt = jnp.minimum(float(w), (pos + 1).astype(F32))
        pooled = s / cnt - ua[:, lanes]
        y = jnp.dot(pooled.astype(BF16), plin_ref[g].astype(BF16), preferred_element_type=F32)
        bra_ref[:, lanes] = (y * pscale_ref[:, lanes]).astype(bra_ref.dtype)
    npool_ref[...] = exta[pl.ds(pad_a + tile - pool_ctx, pool_ctx), :]
    exta[pl.ds(0, pad_a), :] = exta[pl.ds(tile, pad_a), :]

    extb[pl.ds(pad_b, tile), :] = bv_ref[...] * jax.nn.sigmoid(bg_ref[...])
    for r0 in range(0, tile, _ROW_CHUNK):
        for c0 in range(0, bw, _LANE_CHUNK):
            lanes = slice(c0, c0 + _LANE_CHUNK)
            acc = jnp.broadcast_to(cbb_ref[:, lanes], (_ROW_CHUNK, _LANE_CHUNK))
            for k in range(kb):
                acc = acc + cbw_ref[k:k + 1, lanes] * extb[pl.ds(pad_b - (kb - 1) + k + r0, _ROW_CHUNK), lanes]
            conv_scr[pl.ds(r0, _ROW_CHUNK), lanes] = acc
    brb_ref[...] = _layernorm_silu(conv_scr[...], lng_ref[...], lnb_ref[...]).astype(brb_ref.dtype)
    ncb_ref[...] = extb[pl.ds(pad_b + tile - (kb - 1), kb - 1), :]
    extb[pl.ds(0, pad_b), :] = extb[pl.ds(tile, pad_b), :]

    extc[pl.ds(pad_c, tile), :] = ccg_ref[...] * ch_ref[...]
    conv = ccw_ref[0:1, :] * extc[pl.ds(pad_c - (kc - 1), tile), :]
    for k in range(1, kc):
        conv = conv + ccw_ref[k:k + 1, :] * extc[pl.ds(pad_c - (kc - 1) + k, tile), :]
    brc_ref[...] = (cbg_ref[...] * conv).astype(brc_ref.dtype)
    ncc_ref[...] = extc[pl.ds(pad_c + tile - (kc - 1), kc - 1), :]
    extc[pl.ds(0, pad_c), :] = extc[pl.ds(tile, pad_c), :]

    outs = _attend(q_ref[...], lambda h: k_ref[:, h * hd:(h + 1) * hd],
                   lambda h: v_ref[:, h * hd:(h + 1) * hd], n_heads, hd)
    for h, o in enumerate(outs):
        brx_ref[:, h * hd:(h + 1) * hd] = o.astype(brx_ref.dtype)


def _round_up(x, m):
    return (x + m - 1) // m * m


def _prompt_mixers(z, kv, mixp, *, m_total, batch, seq, bw, n_mem, n_heads, pool_ctx):
    plin, pscale, cbw, cbb, lng, lnb, ccw = mixp
    kb, kc = cbw.shape[0], ccw.shape[0]
    tile = _largest_divisor(seq, 256, _ROW_CHUNK)
    nt = seq // tile
    pad_a = _round_up(pool_ctx, _SUBLANES)
    pad_b = _round_up(kb - 1, _SUBLANES)
    pad_c = _round_up(kc - 1, _SUBLANES)

    def zspec(col):
        return pl.BlockSpec((tile, bw), lambda b, t: (b * nt + t, col))

    def full(a):
        return pl.BlockSpec(a.shape, lambda b, t: (0,) * a.ndim)

    row_spec = pl.BlockSpec((tile, bw), lambda b, t: (b * nt + t, 0))
    vmem = (2 * 7 * tile * bw * 4 + 2 * 2 * n_mem * bw * 4 + 2 * 4 * tile * bw * 2
            + (pad_a + pad_b + pad_c + 4 * tile) * bw * 4 + 16 * tile * bw * 4 + 8 * 2**20)
    kern = functools.partial(_prompt_mixer_kernel, tile=tile, pad_a=pad_a, pad_b=pad_b, pad_c=pad_c,
                             n_heads=n_heads)
    return pl.pallas_call(
        kern,
        out_shape=[jax.ShapeDtypeStruct((m_total, bw), BF16)] * 4 + [
            jax.ShapeDtypeStruct((batch, pool_ctx, bw), F32),
            jax.ShapeDtypeStruct((batch, kb - 1, bw), F32),
            jax.ShapeDtypeStruct((batch, kc - 1, bw), F32)],
        grid=(batch, nt),
        in_specs=[zspec(c) for c in range(7)] + [
            pl.BlockSpec((n_mem, bw), lambda b, t: (b, 0)),
            pl.BlockSpec((n_mem, bw), lambda b, t: (b, 1)),
            full(plin), full(pscale), full(cbw), full(cbb), full(lng), full(lnb), full(ccw)],
        out_specs=[row_spec] * 4 + [
            pl.BlockSpec((None, pool_ctx, bw), lambda b, t: (b, 0, 0)),
            pl.BlockSpec((None, kb - 1, bw), lambda b, t: (b, 0, 0)),
            pl.BlockSpec((None, kc - 1, bw), lambda b, t: (b, 0, 0))],
        scratch_shapes=[pltpu.VMEM((pad_a + tile, bw), F32), pltpu.VMEM((pad_b + tile, bw), F32),
                        pltpu.VMEM((pad_c + tile, bw), F32), pltpu.VMEM((tile, bw), F32)],
        compiler_params=_params(("arbitrary", "arbitrary"), vmem),
        name="prompt_mixers",
    )(*([z] * 7), kv, kv, plin, pscale, cbw, cbb, lng, lnb, ccw)


def _sample_mixer_kernel(ua_ref, bv_ref, bg_ref, ch_ref, cbg_ref, ccg_ref, pool_ref, cb_ref, cc_ref,
                         plin_ref, pscale_ref, cbw_ref, cbb_ref, lng_ref, lnb_ref, ccw_ref,
                         bra_in, brb_in, brc_in,
                         bra_ref, brb_ref, brc_ref, npool_ref, ncb_ref, ncc_ref):
    del bra_in, brb_in, brc_in
    steps, sb, bw = ua_ref.shape
    gw = bw // len(POOL_WINDOWS)
    kb = cbw_ref.shape[0]
    kc = ccw_ref.shape[0]
    pool_ctx = pool_ref.shape[0]

    def ext_a(j, lanes):
        return pool_ref[j, :, lanes] if j < pool_ctx else ua_ref[j - pool_ctx, :, lanes]

    for l in range(steps):
        for g, w in enumerate(POOL_WINDOWS):
            lanes = slice(g * gw, (g + 1) * gw)
            tok = ext_a(pool_ctx + l, lanes)
            s = tok
            for i in range(1, w):
                s = s + ext_a(pool_ctx + l - i, lanes)
            cnt = float(min(w, PAST_LEN + l + 1))
            pooled = s / cnt - tok
            y = jnp.dot(pooled.astype(BF16), plin_ref[g].astype(BF16), preferred_element_type=F32)
            bra_ref[l, :, lanes] = (y * pscale_ref[:, lanes]).astype(bra_ref.dtype)
    full = slice(0, bw)
    for j in range(pool_ctx):
        npool_ref[j] = ext_a(steps + j, full)

    glu = [bv_ref[l] * jax.nn.sigmoid(bg_ref[l]) for l in range(steps)]

    def ext_b(j):
        return cb_ref[j] if j < kb - 1 else glu[j - (kb - 1)]

    for l in range(steps):
        acc = jnp.broadcast_to(cbb_ref[...], (sb, bw))
        for k in range(kb):
            acc = acc + cbw_ref[k:k + 1, :] * ext_b(l + k)
        brb_ref[l] = _layernorm_silu(acc, lng_ref[...], lnb_ref[...]).astype(brb_ref.dtype)
    for j in range(kb - 1):
        ncb_ref[j] = ext_b(steps + j)

    cin = [ccg_ref[l] * ch_ref[l] for l in range(steps)]

    def ext_c(j):
        return cc_ref[j] if j < kc - 1 else cin[j - (kc - 1)]

    for l in range(steps):
        conv = ccw_ref[0:1, :] * ext_c(l)
        for k in range(1, kc):
            conv = conv + ccw_ref[k:k + 1, :] * ext_c(l + k)
        brc_ref[l] = (cbg_ref[l] * conv).astype(brc_ref.dtype)
    for j in range(kc - 1):
        ncc_ref[j] = ext_c(steps + j)


def _sample_mixers(z3, states_t, mixp, br3, layer, *, steps, n_seq, bw, row_block0):
    pool_t, cb_t, cc_t = states_t
    plin, pscale, cbw, cbb, lng, lnb, ccw = mixp
    sb = _largest_divisor(n_seq, 32, _BF16_ROWS)
    blk0 = row_block0 // steps

    def zspec(col):
        return pl.BlockSpec((steps, sb, bw), lambda c: (blk0, c, col))

    def sspec(a):
        return pl.BlockSpec((None, a.shape[1], sb, bw), lambda c: (layer, 0, c, 0))

    def full(a):
        return pl.BlockSpec(a.shape, lambda c: (0,) * a.ndim)

    def ospec(a):
        return pl.BlockSpec((a.shape[1], sb, bw), lambda c: (0, c, 0))

    any_spec = pl.BlockSpec(memory_space=pl.ANY)
    ctx_rows = pool_t.shape[1] + cb_t.shape[1] + cc_t.shape[1]
    vmem = (2 * 6 * steps * sb * bw * 4 + 4 * ctx_rows * sb * bw * 4 + 2 * 3 * steps * sb * bw * 2
            + 64 * sb * bw * 4 + 8 * 2**20)
    n_in = 16
    return pl.pallas_call(
        _sample_mixer_kernel,
        out_shape=[jax.ShapeDtypeStruct(b.shape, b.dtype) for b in br3] + [
            jax.ShapeDtypeStruct(pool_t.shape[1:], F32),
            jax.ShapeDtypeStruct(cb_t.shape[1:], F32),
            jax.ShapeDtypeStruct(cc_t.shape[1:], F32)],
        grid=(n_seq // sb,),
        in_specs=[zspec(c) for c in range(6)] + [sspec(pool_t), sspec(cb_t), sspec(cc_t),
                  full(plin), full(pscale), full(cbw), full(cbb), full(lng), full(lnb), full(ccw),
                  any_spec, any_spec, any_spec],
        out_specs=[pl.BlockSpec((steps, sb, bw), lambda c: (blk0, c, 0))] * 3 + [
            pl.BlockSpec((pool_t.shape[1], sb, bw), lambda c: (0, c, 0)),
            pl.BlockSpec((cb_t.shape[1], sb, bw), lambda c: (0, c, 0)),
            pl.BlockSpec((cc_t.shape[1], sb, bw), lambda c: (0, c, 0))],
        input_output_aliases={n_in: 0, n_in + 1: 1, n_in + 2: 2},
        compiler_params=_params(("arbitrary",), vmem),
        name="sample_mixers",
    )(*([z3] * 6), pool_t, cb_t, cc_t, plin, pscale, cbw, cbb, lng, lnb, ccw, *br3)


def _sample_attn_kernel(q_ref, k_ref, v_ref, o_ref, *, steps, n_heads):
    sb, _, bw = k_ref.shape
    hd = bw // n_heads
    group = _SUBLANES // steps
    row_seq = lax.broadcasted_iota(jnp.int32, (_SUBLANES, hd), 0) // steps
    for p in range(sb // group):
        q8 = q_ref[pl.ds(p * _SUBLANES, _SUBLANES), :]
        picked = [None] * n_heads
        for u in range(group):
            s = p * group + u
            outs = _attend(q8, lambda h: k_ref[s, :, h * hd:(h + 1) * hd],
                           lambda h: v_ref[s, :, h * hd:(h + 1) * hd], n_heads, hd)
            for h, o in enumerate(outs):
                picked[h] = o if u == 0 else jnp.where(row_seq == u, o, picked[h])
        for h in range(n_heads):
            o_ref[pl.ds(p * _SUBLANES, _SUBLANES), h * hd:(h + 1) * hd] = picked[h].astype(o_ref.dtype)


def _sample_attention(q, mem_k, mem_v, layer, *, steps, n_heads):
    rows, bw = q.shape
    n_seq = rows // steps
    n_mem = mem_k.shape[2]
    assert _SUBLANES % steps == 0
    sb = _largest_divisor(n_seq, 8, (_BF16_ROWS // steps) or 1)
    kv_spec = pl.BlockSpec((None, sb, n_mem, bw), lambda c: (layer, c, 0, 0))
    vmem = 2 * 2 * sb * n_mem * bw * 4 + 4 * sb * steps * bw * 4 + 8 * 2**20
    return pl.pallas_call(
        functools.partial(_sample_attn_kernel, steps=steps, n_heads=n_heads),
        out_shape=jax.ShapeDtypeStruct((rows, bw), BF16),
        grid=(n_seq // sb,),
        in_specs=[pl.BlockSpec((sb * steps, bw), lambda c: (c, 0)), kv_spec, kv_spec],
        out_specs=pl.BlockSpec((sb * steps, bw), lambda c: (c, 0)),
        compiler_params=_params(("parallel",), vmem),
        name="sample_attention",
    )(q, mem_k, mem_v)


def kernel(x_prompt, x_sample, cache_mem_k, cache_mem_v, state_pool, state_conv_b, state_conv_c, mem_prompt, norm_ff1, w_ff1_in, w_ff1_out, norm_mix, w_in, pool_lin, pool_scale, conv_b_w, conv_b_bias, ln_b_gain, ln_b_bias, conv_c_w, norm_mem, w_mem_kv, w_branch, w_o, norm_ff2, w_ff2_in, w_ff2_out, norm_final):
    batch, seq, d = x_prompt.shape
    n_seq, steps, _ = x_sample.shape
    depth = norm_ff1.shape[0]
    bw = d // N_BRANCH
    n_mem, n_heads, hd = cache_mem_k.shape[2:]
    pool_ctx = state_pool.shape[2]
    mp = batch * seq
    ms = n_seq * steps
    m = mp + ms
    n_z = 7 * bw

    x = jnp.concatenate([x_prompt.reshape(mp, d), x_sample.transpose(1, 0, 2).reshape(ms, d)], axis=0)
    mem = mem_prompt.reshape(batch * n_mem, d)
    states_t = (state_pool.transpose(0, 2, 1, 3), state_conv_b.transpose(0, 2, 1, 3),
                state_conv_c.transpose(0, 2, 1, 3))
    cache_k = cache_mem_k.reshape(depth, n_seq, n_mem, bw)
    cache_v = cache_mem_v.reshape(depth, n_seq, n_mem, bw)

    mk_p, mv_p, pool_p, cb_p, cc_p, pool_s, cb_s, cc_s = [], [], [], [], [], [], [], []
    for l in range(depth):
        row = lambda a: a[l].reshape(1, -1)
        mixp = (pool_lin[l], row(pool_scale), conv_b_w[l], row(conv_b_bias), row(ln_b_gain), row(ln_b_bias),
                conv_c_w[l])

        h = _rmsnorm(x, row(norm_ff1), BF16)
        x = _matmul_residual(_matmul_swiglu(h, w_ff1_in, l), w_ff1_out, l, x, 0.5)

        h = _rmsnorm(x, row(norm_mix), BF16)
        z = _matmul(h, w_in, l, 0, n_z, out_dtype=F32)
        gates = _matmul(h, w_in, l, n_z, N_BRANCH * d, out_dtype=BF16, sigmoid=True)

        kv = _matmul(_rmsnorm(mem, row(norm_mem), BF16), w_mem_kv, l, 0, 2 * bw, out_dtype=F32)
        mk_p.append(kv[:, :bw].reshape(batch, n_mem, n_heads, hd))
        mv_p.append(kv[:, bw:].reshape(batch, n_mem, n_heads, hd))

        bra, brb, brc, brx, npool, ncb, ncc = _prompt_mixers(
            z, kv, mixp, m_total=m, batch=batch, seq=seq, bw=bw, n_mem=n_mem, n_heads=n_heads,
            pool_ctx=pool_ctx)
        pool_p.append(npool); cb_p.append(ncb); cc_p.append(ncc)

        view3 = lambda a: a.reshape(m // n_seq, n_seq, a.shape[-1])
        bra, brb, brc, npool, ncb, ncc = _sample_mixers(
            view3(z), states_t, mixp, [view3(bra), view3(brb), view3(brc)], l,
            steps=steps, n_seq=n_seq, bw=bw, row_block0=mp // n_seq)
        pool_s.append(npool); cb_s.append(ncb); cc_s.append(ncc)

        q_seq = z[mp:, 6 * bw:].reshape(steps, n_seq, bw).transpose(1, 0, 2).reshape(ms, bw)
        o_seq = _sample_attention(q_seq, cache_k, cache_v, l, steps=steps, n_heads=n_heads)
        o_step = o_seq.reshape(n_seq, steps, bw).transpose(1, 0, 2).reshape(ms, bw)
        brx = lax.dynamic_update_slice(brx, o_step, (mp, 0))

        flat = lambda a: a.reshape(m, bw)
        merged = _merge([flat(bra), flat(brb), flat(brc), brx], gates, w_branch, l, d)
        x = _matmul_residual(merged, w_o, l, x, 1.0)

        h = _rmsnorm(x, row(norm_ff2), BF16)
        x = _matmul_residual(_matmul_swiglu(h, w_ff2_in, l), w_ff2_out, l, x, 0.5)

    y = _rmsnorm(x, norm_final.reshape(1, d), F32)
    y_prompt = y[:mp].reshape(batch, seq, d)
    y_sample = y[mp:].reshape(steps, n_seq, d).transpose(1, 0, 2)
    unstep = lambda xs: jnp.stack(xs).transpose(0, 2, 1, 3)
    return (y_prompt, y_sample, jnp.stack(mk_p), jnp.stack(mv_p), jnp.stack(pool_p), jnp.stack(cb_p),
            jnp.stack(cc_p), unstep(pool_s), unstep(cb_s), unstep(cc_s))
```

```python
import functools

import jax
import jax.numpy as jnp
from jax import lax
from jax.experimental import pallas as pl
from jax.experimental.pallas import tpu as pltpu

EPS = 1e-6
PAST_LEN = 16384
POOL_WINDOWS = (2, 4, 8, 16)
N_BRANCH = 4

BF16 = jnp.bfloat16
F32 = jnp.float32

_V7X_VMEM_BYTES = 64 * 2**20
_VMEM_CAP = _V7X_VMEM_BYTES - 6 * 2**20
_SUBLANES = 8
_LANES = 128
_BF16_ROWS = 16
_MXU_COLS = 256
_ACC_ROWS = 272


def _params(semantics, vmem_bytes):
    return pltpu.CompilerParams(dimension_semantics=semantics,
                                vmem_limit_bytes=int(min(_VMEM_CAP, vmem_bytes)))


def _largest_divisor(n, limit, multiple):
    for d in range(min(n, limit), 0, -1):
        if n % d == 0 and d % multiple == 0:
            return d
    raise ValueError(f"no block size for {n} (limit {limit}, multiple {multiple})")


def _round_up(x, m):
    return (x + m - 1) // m * m


def _rmsnorm_kernel(x_ref, g_ref, o_ref):
    x = x_ref[...]
    r = lax.rsqrt(jnp.mean(x * x, axis=-1, keepdims=True) + EPS)
    o_ref[...] = ((x * r) * g_ref[...]).astype(o_ref.dtype)


def _rmsnorm(x, gain, out_dtype, row0=0, rows=None):
    d = x.shape[1]
    rows = x.shape[0] if rows is None else rows
    bm = _largest_divisor(rows, 512, _BF16_ROWS)
    assert row0 % bm == 0
    blk0 = row0 // bm
    osz = jnp.dtype(out_dtype).itemsize
    vmem = 2 * bm * d * (4 + osz) + 4 * bm * d * 4
    return pl.pallas_call(
        _rmsnorm_kernel,
        out_shape=jax.ShapeDtypeStruct((rows, d), out_dtype),
        grid=(rows // bm,),
        in_specs=[pl.BlockSpec((bm, d), lambda i: (blk0 + i, 0)),
                  pl.BlockSpec((1, d), lambda i: (0, 0))],
        out_specs=pl.BlockSpec((bm, d), lambda i: (i, 0)),
        compiler_params=_params(("parallel",), vmem),
        name="rmsnorm",
    )(x, gain)


def _row_scale(acc, rinv):
    n = acc.shape[1]
    return jnp.concatenate([acc[:, c:c + _LANES] * rinv for c in range(0, n, _LANES)], axis=1)


def _row_chunks(ref, chunk):
    return [pl.ds(r0, chunk) for r0 in range(0, ref.shape[0], chunk)]


def _proj_kernel(x_ref, w_ref, *rest, chunk, scaled, sigmoid):
    rinv_ref, o_ref = rest if scaled else (None, rest[0])
    w = w_ref[...].astype(BF16)
    for rows in _row_chunks(x_ref, chunk):
        acc = jnp.dot(x_ref[rows, :], w, preferred_element_type=F32)
        if scaled:
            acc = _row_scale(acc, rinv_ref[rows, :])
        if sigmoid:
            acc = jax.nn.sigmoid(acc)
        o_ref[rows, :] = acc.astype(o_ref.dtype)


def _swiglu_kernel(x_ref, wa_ref, wb_ref, *rest, chunk, scaled):
    rinv_ref, o_ref = rest if scaled else (None, rest[0])
    wa = wa_ref[...].astype(BF16)
    wb = wb_ref[...].astype(BF16)
    for rows in _row_chunks(x_ref, chunk):
        x = x_ref[rows, :]
        a = jnp.dot(x, wa, preferred_element_type=F32)
        b = jnp.dot(x, wb, preferred_element_type=F32)
        if scaled:
            rinv = rinv_ref[rows, :]
            a = _row_scale(a, rinv)
            b = _row_scale(b, rinv)
        o_ref[rows, :] = (jax.nn.silu(a) * b).astype(o_ref.dtype)


def _residual_kernel(x_ref, w_ref, res_ref, *rest, chunk, scale, emit_norm, d_total):
    if emit_norm:
        g_ref, o_ref, og_ref, rinv_ref, ssq_ref = rest
        j = pl.program_id(1)

        @pl.when(j == 0)
        def _():
            ssq_ref[...] = jnp.zeros(ssq_ref.shape, F32)
    else:
        o_ref, = rest
    w = w_ref[...].astype(BF16)
    n = o_ref.shape[1]
    for rows in _row_chunks(x_ref, chunk):
        xn = res_ref[rows, :] + scale * jnp.dot(x_ref[rows, :], w, preferred_element_type=F32)
        o_ref[rows, :] = xn
        if emit_norm:
            og_ref[rows, :] = (xn * g_ref[...]).astype(og_ref.dtype)
            sq = xn * xn
            part = sq[:, 0:_LANES]
            for c in range(_LANES, n, _LANES):
                part = part + sq[:, c:c + _LANES]
            ssq_ref[rows, :] += part
    if emit_norm:
        @pl.when(j == pl.num_programs(1) - 1)
        def _():
            ms = jnp.sum(ssq_ref[...], axis=-1, keepdims=True) / d_total
            rinv_ref[...] = jnp.broadcast_to(lax.rsqrt(ms + EPS), rinv_ref.shape)


def _mm_blocks(m, k, n, n_weights, out_bytes_per_col):
    bm = _largest_divisor(m, max(_BF16_ROWS, (18 * 2**20) // (2 * k)), _BF16_ROWS)
    chunk = _largest_divisor(bm, _ACC_ROWS, _BF16_ROWS)

    def estimate(bn):
        return (bm * k * 2 + n_weights * (2 * k * bn * 4 + k * bn * 2)
                + 2 * bm * bn * out_bytes_per_col + 2 * bm * _LANES * 4 + 8 * chunk * bn * 4 * n_weights)

    bn = 2 * _MXU_COLS if n % (2 * _MXU_COLS) == 0 and estimate(2 * _MXU_COLS) <= 54 * 2**20 else _MXU_COLS
    return bm, bn, chunk, estimate(bn) + 4 * 2**20


def _x_spec(bm, k):
    return pl.BlockSpec((bm, k), lambda i, j: (i, 0), pipeline_mode=pl.Buffered(1))


def _w_spec(k, bn, layer, block0):
    return pl.BlockSpec((None, k, bn), lambda i, j: (layer, 0, block0 + j))


def _rinv_spec(bm):
    return pl.BlockSpec((bm, _LANES), lambda i, j: (i, 0), pipeline_mode=pl.Buffered(1))


def _matmul(x, w_stack, layer, col0, n, *, out_dtype, rinv=None, sigmoid=False):
    m, k = x.shape
    bm, bn, chunk, vmem = _mm_blocks(m, k, n, 1, jnp.dtype(out_dtype).itemsize)
    assert col0 % bn == 0
    scaled = rinv is not None
    return pl.pallas_call(
        functools.partial(_proj_kernel, chunk=chunk, scaled=scaled, sigmoid=sigmoid),
        out_shape=jax.ShapeDtypeStruct((m, n), out_dtype),
        grid=(m // bm, n // bn),
        in_specs=[_x_spec(bm, k), _w_spec(k, bn, layer, col0 // bn)] + ([_rinv_spec(bm)] if scaled else []),
        out_specs=pl.BlockSpec((bm, bn), lambda i, j: (i, j)),
        compiler_params=_params(("parallel", "arbitrary"), vmem),
        name="proj_sigmoid" if sigmoid else "proj",
    )(x, w_stack, *([rinv] if scaled else []))


def _matmul_swiglu(x, w_stack, layer, rinv=None):
    m, k = x.shape
    f = w_stack.shape[-1] // 2
    bm, bn, chunk, vmem = _mm_blocks(m, k, f, 2, 2)
    chunk = _largest_divisor(bm, 2 * _ACC_ROWS, _BF16_ROWS)
    scaled = rinv is not None
    return pl.pallas_call(
        functools.partial(_swiglu_kernel, chunk=chunk, scaled=scaled),
        out_shape=jax.ShapeDtypeStruct((m, f), BF16),
        grid=(m // bm, f // bn),
        in_specs=[_x_spec(bm, k), _w_spec(k, bn, layer, 0), _w_spec(k, bn, layer, f // bn)]
                 + ([_rinv_spec(bm)] if scaled else []),
        out_specs=pl.BlockSpec((bm, bn), lambda i, j: (i, j)),
        compiler_params=_params(("parallel", "arbitrary"), vmem),
        name="ffn_in_swiglu",
    )(x, w_stack, w_stack, *([rinv] if scaled else []))


def _matmul_residual(x, w_stack, layer, res, scale, next_gain=None):
    m, k = x.shape
    n = w_stack.shape[-1]
    emit_norm = next_gain is not None
    bm, bn, chunk, vmem = _mm_blocks(m, k, n, 1, 4 + 4 + (2 if emit_norm else 0))
    bn = _MXU_COLS
    blk = pl.BlockSpec((bm, bn), lambda i, j: (i, j))
    in_specs = [_x_spec(bm, k), _w_spec(k, bn, layer, 0), blk]
    out_shape = [jax.ShapeDtypeStruct((m, n), F32)]
    out_specs = [blk]
    scratch = []
    args = [x, w_stack, res]
    if emit_norm:
        in_specs.append(pl.BlockSpec((1, bn), lambda i, j: (0, j)))
        args.append(next_gain)
        out_shape += [jax.ShapeDtypeStruct((m, n), BF16), jax.ShapeDtypeStruct((m, _LANES), F32)]
        out_specs += [blk, pl.BlockSpec((bm, _LANES), lambda i, j: (i, 0))]
        scratch = [pltpu.VMEM((bm, _LANES), F32)]
    out = pl.pallas_call(
        functools.partial(_residual_kernel, chunk=chunk, scale=scale, emit_norm=emit_norm, d_total=n),
        out_shape=out_shape,
        grid=(m // bm, n // bn),
        in_specs=in_specs,
        out_specs=out_specs,
        scratch_shapes=scratch,
        compiler_params=_params(("parallel", "arbitrary"), vmem),
        name="proj_residual",
    )(*args)
    return out if emit_norm else (out[0], None, None)


def _merge_kernel(*refs, chunk):
    br_refs, g_refs, w_refs, o_ref = refs[0:4], refs[4:8], refs[8:12], refs[12]
    ws = [w_ref[...].astype(BF16) for w_ref in w_refs]
    for rows in _row_chunks(o_ref, chunk):
        acc = None
        for br_ref, g_ref, w in zip(br_refs, g_refs, ws):
            term = g_ref[rows, :].astype(F32) * jnp.dot(br_ref[rows, :], w, preferred_element_type=F32)
            acc = term if acc is None else acc + term
        o_ref[rows, :] = acc.astype(o_ref.dtype)


def _merge(branches, gates, w_branch, layer, d):
    m, bw = branches[0].shape
    bn = _MXU_COLS
    bm = _largest_divisor(m, (18 * 2**20) // (2 * N_BRANCH * bw), _BF16_ROWS)
    chunk = _largest_divisor(bm, _ACC_ROWS, _BF16_ROWS)
    nb = d // bn
    vmem = (N_BRANCH * bm * bw * 2 + N_BRANCH * 2 * bw * bn * 4 + N_BRANCH * bw * bn * 2
            + N_BRANCH * 2 * bm * bn * 2 + 2 * bm * bn * 2 + 12 * chunk * bn * 4 + 4 * 2**20)
    br_spec = pl.BlockSpec((bm, bw), lambda i, j: (i, 0), pipeline_mode=pl.Buffered(1))
    gate_specs = [pl.BlockSpec((bm, bn), functools.partial(lambda i, j, n: (i, n * nb + j), n=n))
                  for n in range(N_BRANCH)]
    w_specs = [pl.BlockSpec((None, None, bw, bn), functools.partial(lambda i, j, n: (layer, n, 0, j), n=n))
               for n in range(N_BRANCH)]
    return pl.pallas_call(
        functools.partial(_merge_kernel, chunk=chunk),
        out_shape=jax.ShapeDtypeStruct((m, d), BF16),
        grid=(m // bm, nb),
        in_specs=[br_spec] * N_BRANCH + gate_specs + w_specs,
        out_specs=pl.BlockSpec((bm, bn), lambda i, j: (i, j)),
        compiler_params=_params(("parallel", "arbitrary"), vmem),
        name="gated_merge",
    )(*branches, *([gates] * N_BRANCH), *([w_branch] * N_BRANCH))


def _layernorm_silu(y, gain, bias):
    mu = jnp.mean(y, axis=-1, keepdims=True)
    c = y - mu
    var = jnp.mean(c * c, axis=-1, keepdims=True)
    return jax.nn.silu(c * lax.rsqrt(var + EPS) * gain + bias)


def _softmax(s):
    e = jnp.exp(s - jnp.max(s, axis=-1, keepdims=True))
    return e / jnp.sum(e, axis=-1, keepdims=True)


_ROW_CHUNK = 64
_LANE_CHUNK = 256


def _prompt_mixer_kernel(ua_ref, bv_ref, bg_ref, ch_ref, cbg_ref, ccg_ref, q_ref, k_ref, v_ref,
                         plin_ref, pscale_ref, cbw_ref, cbb_ref, lng_ref, lnb_ref, ccw_ref,
                         bra_ref, brb_ref, brc_ref, brx_ref, npool_ref, ncb_ref, ncc_ref,
                         exta, extb, extc, shifted, conv_scr,
                         *, tile, pad_a, pad_b, pad_c, n_heads):
    t = pl.program_id(1)
    bw = ua_ref.shape[-1]
    gw = bw // len(POOL_WINDOWS)
    hd = bw // n_heads
    kb = cbw_ref.shape[0]
    kc = ccw_ref.shape[0]
    pool_ctx = npool_ref.shape[0]

    @pl.when(t == 0)
    def _():
        exta[pl.ds(0, pad_a), :] = jnp.zeros((pad_a, bw), F32)
        extb[pl.ds(0, pad_b), :] = jnp.zeros((pad_b, bw), F32)
        extc[pl.ds(0, pad_c), :] = jnp.zeros((pad_c, bw), F32)

    ua = ua_ref[...]
    exta[pl.ds(pad_a, tile), :] = ua
    pos = t * tile + lax.broadcasted_iota(jnp.int32, (tile, 1), 0)
    for g, w in enumerate(POOL_WINDOWS):
        lanes = slice(g * gw, (g + 1) * gw)
        s = ua[:, lanes]
        for i in range(1, w):
            s = s + exta[pl.ds(pad_a - i, tile), lanes]
        cnt = jnp.minimum(float(w), (pos + 1).astype(F32))
        pooled = s / cnt - ua[:, lanes]
        y = jnp.dot(pooled.astype(BF16), plin_ref[g].astype(BF16), preferred_element_type=F32)
        bra_ref[:, lanes] = (y * pscale_ref[:, lanes]).astype(bra_ref.dtype)
    npool_ref[...] = exta[pl.ds(pad_a + tile - pool_ctx, pool_ctx), :]
    exta[pl.ds(0, pad_a), :] = exta[pl.ds(tile, pad_a), :]

    extb[pl.ds(pad_b, tile), :] = bv_ref[...] * jax.nn.sigmoid(bg_ref[...])
    lo = _SUBLANES
    span = pad_b + tile - lo
    for s in range(1, _SUBLANES):
        shifted[s - 1, pl.ds(lo, span), :] = extb[pl.ds(lo - s, span), :]
    for r0 in range(0, tile, _ROW_CHUNK):
        for c0 in range(0, bw, _LANE_CHUNK):
            lanes = slice(c0, c0 + _LANE_CHUNK)
            acc = jnp.broadcast_to(cbb_ref[:, lanes], (_ROW_CHUNK, _LANE_CHUNK))
            for back in range(kb):
                a, s = divmod(back, _SUBLANES)
                rows = pl.ds(pad_b + r0 - _SUBLANES * a, _ROW_CHUNK)
                src = extb[rows, lanes] if s == 0 else shifted[s - 1, rows, lanes]
                acc = acc + cbw_ref[kb - 1 - back:kb - back, lanes] * src
            conv_scr[pl.ds(r0, _ROW_CHUNK), lanes] = acc
    brb_ref[...] = _layernorm_silu(conv_scr[...], lng_ref[...], lnb_ref[...]).astype(brb_ref.dtype)
    ncb_ref[...] = extb[pl.ds(pad_b + tile - (kb - 1), kb - 1), :]
    extb[pl.ds(0, pad_b), :] = extb[pl.ds(tile, pad_b), :]

    extc[pl.ds(pad_c, tile), :] = ccg_ref[...] * ch_ref[...]
    conv = ccw_ref[0:1, :] * extc[pl.ds(pad_c - (kc - 1), tile), :]
    for k in range(1, kc):
        conv = conv + ccw_ref[k:k + 1, :] * extc[pl.ds(pad_c - (kc - 1) + k, tile), :]
    brc_ref[...] = (cbg_ref[...] * conv).astype(brc_ref.dtype)
    ncc_ref[...] = extc[pl.ds(pad_c + tile - (kc - 1), kc - 1), :]
    extc[pl.ds(0, pad_c), :] = extc[pl.ds(tile, pad_c), :]

    q = q_ref[...]
    for h in range(n_heads):
        cols = slice(h * hd, (h + 1) * hd)
        s = lax.dot_general(q[:, cols].astype(BF16), k_ref[:, cols].astype(BF16), (((1,), (1,)), ((), ())),
                            preferred_element_type=F32) * (hd ** -0.5)
        o = jnp.dot(_softmax(s).astype(BF16), v_ref[:, cols].astype(BF16), preferred_element_type=F32)
        brx_ref[:, cols] = o.astype(brx_ref.dtype)


def _prompt_mixers(z, kv, mixp, *, m_total, batch, seq, bw, n_mem, n_heads, pool_ctx):
    plin, pscale, cbw, cbb, lng, lnb, ccw = mixp
    kb, kc = cbw.shape[0], ccw.shape[0]
    tile = _largest_divisor(seq, 256, _ROW_CHUNK)
    nt = seq // tile
    pad_a = _round_up(pool_ctx, _SUBLANES)
    pad_b = _round_up(kb - 1, _SUBLANES) + _SUBLANES
    pad_c = _round_up(kc - 1, _SUBLANES)

    def zspec(col):
        return pl.BlockSpec((tile, bw), lambda b, t: (b * nt + t, col))

    def full(a):
        return pl.BlockSpec(a.shape, lambda b, t: (0,) * a.ndim)

    row_spec = pl.BlockSpec((tile, bw), lambda b, t: (b * nt + t, 0))
    vmem = (2 * 7 * tile * bw * 4 + 2 * 2 * n_mem * bw * 4 + 2 * 4 * tile * bw * 2
            + (pad_a + _SUBLANES * pad_b + pad_c + 11 * tile) * bw * 4 + 16 * tile * bw * 4 + 8 * 2**20)
    kern = functools.partial(_prompt_mixer_kernel, tile=tile, pad_a=pad_a, pad_b=pad_b, pad_c=pad_c,
                             n_heads=n_heads)
    return pl.pallas_call(
        kern,
        out_shape=[jax.ShapeDtypeStruct((m_total, bw), BF16)] * 4 + [
            jax.ShapeDtypeStruct((batch, pool_ctx, bw), F32),
            jax.ShapeDtypeStruct((batch, kb - 1, bw), F32),
            jax.ShapeDtypeStruct((batch, kc - 1, bw), F32)],
        grid=(batch, nt),
        in_specs=[zspec(c) for c in range(7)] + [
            pl.BlockSpec((n_mem, bw), lambda b, t: (b, 0)),
            pl.BlockSpec((n_mem, bw), lambda b, t: (b, 1)),
            full(plin), full(pscale), full(cbw), full(cbb), full(lng), full(lnb), full(ccw)],
        out_specs=[row_spec] * 4 + [
            pl.BlockSpec((None, pool_ctx, bw), lambda b, t: (b, 0, 0)),
            pl.BlockSpec((None, kb - 1, bw), lambda b, t: (b, 0, 0)),
            pl.BlockSpec((None, kc - 1, bw), lambda b, t: (b, 0, 0))],
        scratch_shapes=[pltpu.VMEM((pad_a + tile, bw), F32), pltpu.VMEM((pad_b + tile, bw), F32),
                        pltpu.VMEM((pad_c + tile, bw), F32),
                        pltpu.VMEM((_SUBLANES - 1, pad_b + tile, bw), F32), pltpu.VMEM((tile, bw), F32)],
        compiler_params=_params(("arbitrary", "arbitrary"), vmem),
        name="prompt_mixers",
    )(*([z] * 7), kv, kv, plin, pscale, cbw, cbb, lng, lnb, ccw)


def _sample_mixer_kernel(ua_ref, bv_ref, bg_ref, ch_ref, cbg_ref, ccg_ref, pool_ref, cb_ref, cc_ref,
                         plin_ref, pscale_ref, cbw_ref, cbb_ref, lng_ref, lnb_ref, ccw_ref,
                         bra_in, brb_in, brc_in,
                         bra_ref, brb_ref, brc_ref, npool_ref, ncb_ref, ncc_ref):
    del bra_in, brb_in, brc_in
    steps, sb, bw = ua_ref.shape
    gw = bw // len(POOL_WINDOWS)
    kb = cbw_ref.shape[0]
    kc = ccw_ref.shape[0]
    pool_ctx = pool_ref.shape[0]

    def ext_a(j, lanes):
        return pool_ref[j, :, lanes] if j < pool_ctx else ua_ref[j - pool_ctx, :, lanes]

    for l in range(steps):
        for g, w in enumerate(POOL_WINDOWS):
            lanes = slice(g * gw, (g + 1) * gw)
            tok = ext_a(pool_ctx + l, lanes)
            s = tok
            for i in range(1, w):
                s = s + ext_a(pool_ctx + l - i, lanes)
            cnt = float(min(w, PAST_LEN + l + 1))
            pooled = s / cnt - tok
            y = jnp.dot(pooled.astype(BF16), plin_ref[g].astype(BF16), preferred_element_type=F32)
            bra_ref[l, :, lanes] = (y * pscale_ref[:, lanes]).astype(bra_ref.dtype)
    full = slice(0, bw)
    for j in range(pool_ctx):
        npool_ref[j] = ext_a(steps + j, full)

    glu = [bv_ref[l] * jax.nn.sigmoid(bg_ref[l]) for l in range(steps)]

    def ext_b(j):
        return cb_ref[j] if j < kb - 1 else glu[j - (kb - 1)]

    for l in range(steps):
        acc = jnp.broadcast_to(cbb_ref[...], (sb, bw))
        for k in range(kb):
            acc = acc + cbw_ref[k:k + 1, :] * ext_b(l + k)
        brb_ref[l] = _layernorm_silu(acc, lng_ref[...], lnb_ref[...]).astype(brb_ref.dtype)
    for j in range(kb - 1):
        ncb_ref[j] = ext_b(steps + j)

    cin = [ccg_ref[l] * ch_ref[l] for l in range(steps)]

    def ext_c(j):
        return cc_ref[j] if j < kc - 1 else cin[j - (kc - 1)]

    for l in range(steps):
        conv = ccw_ref[0:1, :] * ext_c(l)
        for k in range(1, kc):
            conv = conv + ccw_ref[k:k + 1, :] * ext_c(l + k)
        brc_ref[l] = (cbg_ref[l] * conv).astype(brc_ref.dtype)
    for j in range(kc - 1):
        ncc_ref[j] = ext_c(steps + j)


def _sample_mixers(z3, states_t, mixp, br3, layer, *, steps, n_seq, bw, row_block0):
    pool_t, cb_t, cc_t = states_t
    plin, pscale, cbw, cbb, lng, lnb, ccw = mixp
    sb = _largest_divisor(n_seq, 32, _BF16_ROWS)
    blk0 = row_block0 // steps

    def zspec(col):
        return pl.BlockSpec((steps, sb, bw), lambda c: (blk0, c, col))

    def sspec(a):
        return pl.BlockSpec((None, a.shape[1], sb, bw), lambda c: (layer, 0, c, 0))

    def full(a):
        return pl.BlockSpec(a.shape, lambda c: (0,) * a.ndim)

    any_spec = pl.BlockSpec(memory_space=pl.ANY)
    ctx_rows = pool_t.shape[1] + cb_t.shape[1] + cc_t.shape[1]
    vmem = (2 * 6 * steps * sb * bw * 4 + 4 * ctx_rows * sb * bw * 4 + 2 * 3 * steps * sb * bw * 2
            + 64 * sb * bw * 4 + 8 * 2**20)
    n_in = 16
    return pl.pallas_call(
        _sample_mixer_kernel,
        out_shape=[jax.ShapeDtypeStruct(b.shape, b.dtype) for b in br3] + [
            jax.ShapeDtypeStruct(pool_t.shape[1:], F32),
            jax.ShapeDtypeStruct(cb_t.shape[1:], F32),
            jax.ShapeDtypeStruct(cc_t.shape[1:], F32)],
        grid=(n_seq // sb,),
        in_specs=[zspec(c) for c in range(6)] + [sspec(pool_t), sspec(cb_t), sspec(cc_t),
                  full(plin), full(pscale), full(cbw), full(cbb), full(lng), full(lnb), full(ccw),
                  any_spec, any_spec, any_spec],
        out_specs=[pl.BlockSpec((steps, sb, bw), lambda c: (blk0, c, 0))] * 3 + [
            pl.BlockSpec((pool_t.shape[1], sb, bw), lambda c: (0, c, 0)),
            pl.BlockSpec((cb_t.shape[1], sb, bw), lambda c: (0, c, 0)),
            pl.BlockSpec((cc_t.shape[1], sb, bw), lambda c: (0, c, 0))],
        input_output_aliases={n_in: 0, n_in + 1: 1, n_in + 2: 2},
        compiler_params=_params(("arbitrary",), vmem),
        name="sample_mixers",
    )(*([z3] * 6), pool_t, cb_t, cc_t, plin, pscale, cbw, cbb, lng, lnb, ccw, *br3)


def _sample_attn_kernel(q_ref, k_ref, v_ref, o_ref, *, steps, n_heads, n_mem):
    sb = k_ref.shape[0]
    hd = q_ref.shape[-1] // n_heads
    chunks = hd // _LANES
    rows_per_token = chunks * n_heads
    group = _SUBLANES // steps

    def kv_slice(ref, s, h, c):
        return ref[s, pl.ds(c * n_heads + h, n_mem, stride=rows_per_token), :].astype(BF16)

    scores = []
    for s in range(sb):
        q8 = q_ref[pl.ds((s // group) * _SUBLANES, _SUBLANES), :]
        for h in range(n_heads):
            acc = None
            for c in range(chunks):
                lo = h * hd + c * _LANES
                d = lax.dot_general(q8[:, lo:lo + _LANES].astype(BF16), kv_slice(k_ref, s, h, c),
                                    (((1,), (1,)), ((), ())), preferred_element_type=F32)
                acc = d if acc is None else acc + d
            scores.append(acc)
    p = _softmax(jnp.concatenate(scores, axis=0) * (hd ** -0.5)).astype(BF16)
    row_seq = lax.broadcasted_iota(jnp.int32, (_SUBLANES, _LANES), 0) // steps
    for pair in range(sb // group):
        for h in range(n_heads):
            for c in range(chunks):
                picked = None
                for u in range(group):
                    s = pair * group + u
                    i = s * n_heads + h
                    o = jnp.dot(p[i * _SUBLANES:(i + 1) * _SUBLANES], kv_slice(v_ref, s, h, c),
                                preferred_element_type=F32)
                    picked = o if u == 0 else jnp.where(row_seq == u, o, picked)
                lo = h * hd + c * _LANES
                o_ref[pl.ds(pair * _SUBLANES, _SUBLANES), lo:lo + _LANES] = picked.astype(o_ref.dtype)


def _sample_attention(q, mem_k, mem_v, layer, *, steps, n_heads, n_mem):
    rows, bw = q.shape
    n_seq = rows // steps
    assert _SUBLANES % steps == 0 and (bw // n_heads) % _LANES == 0
    sb = _largest_divisor(n_seq, 8, max(1, _BF16_ROWS // steps))
    kv_spec = pl.BlockSpec((None, sb) + mem_k.shape[2:], lambda c: (layer, c, 0, 0))
    vmem = 2 * 2 * sb * n_mem * bw * 4 + 4 * sb * steps * bw * 4 + 8 * 2**20
    return pl.pallas_call(
        functools.partial(_sample_attn_kernel, steps=steps, n_heads=n_heads, n_mem=n_mem),
        out_shape=jax.ShapeDtypeStruct((rows, bw), BF16),
        grid=(n_seq // sb,),
        in_specs=[pl.BlockSpec((sb * steps, bw), lambda c: (c, 0)), kv_spec, kv_spec],
        out_specs=pl.BlockSpec((sb * steps, bw), lambda c: (c, 0)),
        compiler_params=_params(("parallel",), vmem),
        name="sample_attention",
    )(q, mem_k, mem_v)


def kernel(x_prompt, x_sample, cache_mem_k, cache_mem_v, state_pool, state_conv_b, state_conv_c, mem_prompt, norm_ff1, w_ff1_in, w_ff1_out, norm_mix, w_in, pool_lin, pool_scale, conv_b_w, conv_b_bias, ln_b_gain, ln_b_bias, conv_c_w, norm_mem, w_mem_kv, w_branch, w_o, norm_ff2, w_ff2_in, w_ff2_out, norm_final):
    batch, seq, d = x_prompt.shape
    n_seq, steps, _ = x_sample.shape
    depth = norm_ff1.shape[0]
    bw = d // N_BRANCH
    n_mem, n_heads, hd = cache_mem_k.shape[2:]
    pool_ctx = state_pool.shape[2]
    mp = batch * seq
    ms = n_seq * steps
    m = mp + ms
    n_z = 7 * bw
    chunks = hd // _LANES

    x = jnp.concatenate([x_prompt.reshape(mp, d), x_sample.transpose(1, 0, 2).reshape(ms, d)], axis=0)
    mem = mem_prompt.reshape(batch * n_mem, d)
    states_t = (state_pool.transpose(0, 2, 1, 3), state_conv_b.transpose(0, 2, 1, 3),
                state_conv_c.transpose(0, 2, 1, 3))

    def token_chunk_head_rows(cache):
        c = cache.reshape(depth, n_seq, n_mem, n_heads, chunks, _LANES).transpose(0, 1, 2, 4, 3, 5)
        return c.reshape(depth, n_seq, n_mem * chunks * n_heads, _LANES)

    cache_k = token_chunk_head_rows(cache_mem_k)
    cache_v = token_chunk_head_rows(cache_mem_v)

    mk_p, mv_p, pool_p, cb_p, cc_p, pool_s, cb_s, cc_s = [], [], [], [], [], [], [], []
    h = _rmsnorm(x, norm_ff1[0].reshape(1, d), BF16)
    rinv = None
    for l in range(depth):
        row = lambda a: a[l].reshape(1, -1)
        mixp = (pool_lin[l], row(pool_scale), conv_b_w[l], row(conv_b_bias), row(ln_b_gain), row(ln_b_bias),
                conv_c_w[l])

        x, h, rinv = _matmul_residual(_matmul_swiglu(h, w_ff1_in, l, rinv), w_ff1_out, l, x, 0.5, row(norm_mix))

        z = _matmul(h, w_in, l, 0, n_z, out_dtype=F32, rinv=rinv)
        gates = _matmul(h, w_in, l, n_z, N_BRANCH * d, out_dtype=BF16, rinv=rinv, sigmoid=True)

        kv = _matmul(_rmsnorm(mem, row(norm_mem), BF16), w_mem_kv, l, 0, 2 * bw, out_dtype=F32)
        mk_p.append(kv[:, :bw].reshape(batch, n_mem, n_heads, hd))
        mv_p.append(kv[:, bw:].reshape(batch, n_mem, n_heads, hd))

        bra, brb, brc, brx, npool, ncb, ncc = _prompt_mixers(
            z, kv, mixp, m_total=m, batch=batch, seq=seq, bw=bw, n_mem=n_mem, n_heads=n_heads,
            pool_ctx=pool_ctx)
        pool_p.append(npool); cb_p.append(ncb); cc_p.append(ncc)

        view3 = lambda a: a.reshape(m // n_seq, n_seq, a.shape[-1])
        bra, brb, brc, npool, ncb, ncc = _sample_mixers(
            view3(z), states_t, mixp, [view3(bra), view3(brb), view3(brc)], l,
            steps=steps, n_seq=n_seq, bw=bw, row_block0=mp // n_seq)
        pool_s.append(npool); cb_s.append(ncb); cc_s.append(ncc)

        q_seq = z[mp:, 6 * bw:].reshape(steps, n_seq, bw).transpose(1, 0, 2).reshape(ms, bw)
        o_seq = _sample_attention(q_seq, cache_k, cache_v, l, steps=steps, n_heads=n_heads, n_mem=n_mem)
        o_step = o_seq.reshape(n_seq, steps, bw).transpose(1, 0, 2).reshape(ms, bw)
        brx = lax.dynamic_update_slice(brx, o_step, (mp, 0))

        flat = lambda a: a.reshape(m, bw)
        merged = _merge([flat(bra), flat(brb), flat(brc), brx], gates, w_branch, l, d)
        x, h, rinv = _matmul_residual(merged, w_o, l, x, 1.0, row(norm_ff2))

        next_gain = norm_ff1[l + 1].reshape(1, d) if l + 1 < depth else None
        x, h, rinv = _matmul_residual(_matmul_swiglu(h, w_ff2_in, l, rinv), w_ff2_out, l, x, 0.5, next_gain)

    final_gain = norm_final.reshape(1, d)
    y_prompt = _rmsnorm(x, final_gain, F32, 0, mp).reshape(batch, seq, d)
    y_sample = _rmsnorm(x, final_gain, F32, mp, ms).reshape(steps, n_seq, d).transpose(1, 0, 2)
    unstep = lambda xs: jnp.stack(xs).transpose(0, 2, 1, 3)
    return (y_prompt, y_sample, jnp.stack(mk_p), jnp.stack(mv_p), jnp.stack(pool_p), jnp.stack(cb_p),
            jnp.stack(cc_p), unstep(pool_s), unstep(cb_s), unstep(cc_s))
```

```python
import functools

import jax
import jax.numpy as jnp
from jax import lax
from jax.experimental import pallas as pl
from jax.experimental.pallas import tpu as pltpu

EPS = 1e-6
PAST_LEN = 16384
POOL_WINDOWS = (2, 4, 8, 16)
N_BRANCH = 4

BF16 = jnp.bfloat16
F32 = jnp.float32

_V7X_VMEM_BYTES = 64 * 2**20
_VMEM_CAP = _V7X_VMEM_BYTES - 6 * 2**20
_SUBLANES = 8
_LANES = 128
_BF16_ROWS = 16
_MXU_COLS = 256
_ACC_ROWS = 272
_RESIDENT_BYTES = 18 * 2**20


def _params(semantics, vmem_bytes):
    return pltpu.CompilerParams(dimension_semantics=semantics,
                                vmem_limit_bytes=int(min(_VMEM_CAP, vmem_bytes)))


def _largest_divisor(n, limit, multiple):
    for d in range(min(n, limit), 0, -1):
        if n % d == 0 and d % multiple == 0:
            return d
    raise ValueError(f"no block size for {n} (limit {limit}, multiple {multiple})")


def _round_up(x, m):
    return (x + m - 1) // m * m


def _rmsnorm_kernel(x_ref, g_ref, o_ref):
    x = x_ref[...]
    r = lax.rsqrt(jnp.mean(x * x, axis=-1, keepdims=True) + EPS)
    o_ref[...] = ((x * r) * g_ref[...]).astype(o_ref.dtype)


def _rmsnorm(x, gain, out_dtype, row0=0, rows=None):
    d = x.shape[1]
    rows = x.shape[0] if rows is None else rows
    bm = _largest_divisor(rows, 512, _BF16_ROWS)
    assert row0 % bm == 0
    blk0 = row0 // bm
    osz = jnp.dtype(out_dtype).itemsize
    vmem = 2 * bm * d * (4 + osz) + 4 * bm * d * 4
    return pl.pallas_call(
        _rmsnorm_kernel,
        out_shape=jax.ShapeDtypeStruct((rows, d), out_dtype),
        grid=(rows // bm,),
        in_specs=[pl.BlockSpec((bm, d), lambda i: (blk0 + i, 0)),
                  pl.BlockSpec((1, d), lambda i: (0, 0))],
        out_specs=pl.BlockSpec((bm, d), lambda i: (i, 0)),
        compiler_params=_params(("parallel",), vmem),
        name="rmsnorm",
    )(x, gain)


def _row_scale(acc, rinv):
    n = acc.shape[1]
    return jnp.concatenate([acc[:, c:c + _LANES] * rinv for c in range(0, n, _LANES)], axis=1)


def _resident_rows(x_hbms, xbufs, sem, chunk, prepare, compute_chunk):
    i, j = pl.program_id(0), pl.program_id(1)
    ni, nj = pl.num_programs(0), pl.num_programs(1)
    bm = xbufs[0].shape[0]
    chunks = [pl.ds(r0, chunk) for r0 in range(0, bm, chunk)]

    def copies(blk, c):
        src_rows = pl.ds(blk * bm + c * chunk, chunk)
        return [pltpu.make_async_copy(xh.at[src_rows, :], xb.at[chunks[c], :], sem.at[a, c])
                for a, (xh, xb) in enumerate(zip(x_hbms, xbufs))]

    @pl.when((i == 0) & (j == 0))
    def _():
        for c in range(len(chunks)):
            for cp in copies(0, c):
                cp.start()

    @pl.when(j == 0)
    def _():
        prepared = prepare()
        for c, rows in enumerate(chunks):
            for cp in copies(i, c):
                cp.wait()
            compute_chunk(prepared, rows)

    @pl.when(j > 0)
    def _():
        prepared = prepare()
        for rows in chunks:
            compute_chunk(prepared, rows)

    @pl.when((j == nj - 1) & (i + 1 < ni))
    def _():
        for c in range(len(chunks)):
            for cp in copies(i + 1, c):
                cp.start()


def _proj_kernel(x_hbm, w_ref, *rest, chunk, scaled, sigmoid):
    (rinv_ref, o_ref, xbuf, sem) = rest if scaled else (None,) + rest

    def compute_chunk(w, rows):
        acc = jnp.dot(xbuf[rows, :], w, preferred_element_type=F32)
        if scaled:
            acc = _row_scale(acc, rinv_ref[rows, :])
        if sigmoid:
            acc = jax.nn.sigmoid(acc)
        o_ref[rows, :] = acc.astype(o_ref.dtype)

    _resident_rows([x_hbm], [xbuf], sem, chunk, lambda: w_ref[...].astype(BF16), compute_chunk)


def _swiglu_kernel(x_hbm, wa_ref, wb_ref, *rest, chunk, scaled):
    (rinv_ref, o_ref, xbuf, sem) = rest if scaled else (None,) + rest

    def compute_chunk(ws, rows):
        x = xbuf[rows, :]
        a = jnp.dot(x, ws[0], preferred_element_type=F32)
        b = jnp.dot(x, ws[1], preferred_element_type=F32)
        if scaled:
            rinv = rinv_ref[rows, :]
            a = _row_scale(a, rinv)
            b = _row_scale(b, rinv)
        o_ref[rows, :] = (jax.nn.silu(a) * b).astype(o_ref.dtype)

    _resident_rows([x_hbm], [xbuf], sem, chunk,
                   lambda: (wa_ref[...].astype(BF16), wb_ref[...].astype(BF16)), compute_chunk)


def _residual_kernel(x_hbm, w_ref, res_ref, *rest, chunk, scale, emit_norm, d_total):
    if emit_norm:
        g_ref, o_ref, og_ref, rinv_ref, xbuf, sem, ssq_ref = rest
        j = pl.program_id(1)

        @pl.when(j == 0)
        def _():
            ssq_ref[...] = jnp.zeros(ssq_ref.shape, F32)
    else:
        o_ref, xbuf, sem = rest
    n = o_ref.shape[1]

    def compute_chunk(w, rows):
        xn = res_ref[rows, :] + scale * jnp.dot(xbuf[rows, :], w, preferred_element_type=F32)
        o_ref[rows, :] = xn
        if emit_norm:
            og_ref[rows, :] = (xn * g_ref[...]).astype(og_ref.dtype)
            sq = xn * xn
            part = sq[:, 0:_LANES]
            for c in range(_LANES, n, _LANES):
                part = part + sq[:, c:c + _LANES]
            ssq_ref[rows, :] += part

    _resident_rows([x_hbm], [xbuf], sem, chunk, lambda: w_ref[...].astype(BF16), compute_chunk)
    if emit_norm:
        @pl.when(j == pl.num_programs(1) - 1)
        def _():
            ms = jnp.sum(ssq_ref[...], axis=-1, keepdims=True) / d_total
            rinv_ref[...] = jnp.broadcast_to(lax.rsqrt(ms + EPS), rinv_ref.shape)


def _mm_blocks(m, k, n, n_weights, out_bytes_per_col):
    bm = _largest_divisor(m, max(_BF16_ROWS, _RESIDENT_BYTES // (2 * k)), _BF16_ROWS)
    chunk = _largest_divisor(bm, _ACC_ROWS, _BF16_ROWS)

    def estimate(bn):
        return (bm * k * 2 + n_weights * (2 * k * bn * 4 + k * bn * 2)
                + 2 * bm * bn * out_bytes_per_col + 2 * bm * _LANES * 4 + 8 * chunk * bn * 4 * n_weights)

    bn = 2 * _MXU_COLS if n % (2 * _MXU_COLS) == 0 and estimate(2 * _MXU_COLS) <= 54 * 2**20 else _MXU_COLS
    return bm, bn, chunk, estimate(bn) + 4 * 2**20


_HBM_SPEC = pl.BlockSpec(memory_space=pl.ANY)
_SEQUENTIAL = ("arbitrary", "arbitrary")


def _x_scratch(n_arrays, bm, k, chunk):
    return [pltpu.VMEM((bm, k), BF16)] * n_arrays + [pltpu.SemaphoreType.DMA((n_arrays, bm // chunk))]


def _w_spec(k, bn, layer, block0):
    return pl.BlockSpec((None, k, bn), lambda i, j: (layer, 0, block0 + j))


def _rinv_spec(bm):
    return pl.BlockSpec((bm, _LANES), lambda i, j: (i, 0), pipeline_mode=pl.Buffered(1))


def _matmul(x, w_stack, layer, col0, n, *, out_dtype, rinv=None, sigmoid=False):
    m, k = x.shape
    bm, bn, chunk, vmem = _mm_blocks(m, k, n, 1, jnp.dtype(out_dtype).itemsize)
    assert col0 % bn == 0
    scaled = rinv is not None
    return pl.pallas_call(
        functools.partial(_proj_kernel, chunk=chunk, scaled=scaled, sigmoid=sigmoid),
        out_shape=jax.ShapeDtypeStruct((m, n), out_dtype),
        grid=(m // bm, n // bn),
        in_specs=[_HBM_SPEC, _w_spec(k, bn, layer, col0 // bn)] + ([_rinv_spec(bm)] if scaled else []),
        out_specs=pl.BlockSpec((bm, bn), lambda i, j: (i, j)),
        scratch_shapes=_x_scratch(1, bm, k, chunk),
        compiler_params=_params(_SEQUENTIAL, vmem),
        name="proj_sigmoid" if sigmoid else "proj",
    )(x, w_stack, *([rinv] if scaled else []))


def _matmul_swiglu(x, w_stack, layer, rinv=None):
    m, k = x.shape
    f = w_stack.shape[-1] // 2
    bm, bn, chunk, vmem = _mm_blocks(m, k, f, 2, 2)
    chunk = _largest_divisor(bm, 2 * _ACC_ROWS, _BF16_ROWS)
    scaled = rinv is not None
    return pl.pallas_call(
        functools.partial(_swiglu_kernel, chunk=chunk, scaled=scaled),
        out_shape=jax.ShapeDtypeStruct((m, f), BF16),
        grid=(m // bm, f // bn),
        in_specs=[_HBM_SPEC, _w_spec(k, bn, layer, 0), _w_spec(k, bn, layer, f // bn)]
                 + ([_rinv_spec(bm)] if scaled else []),
        out_specs=pl.BlockSpec((bm, bn), lambda i, j: (i, j)),
        scratch_shapes=_x_scratch(1, bm, k, chunk),
        compiler_params=_params(_SEQUENTIAL, vmem),
        name="ffn_in_swiglu",
    )(x, w_stack, w_stack, *([rinv] if scaled else []))


def _matmul_residual(x, w_stack, layer, res, scale, next_gain=None):
    m, k = x.shape
    n = w_stack.shape[-1]
    emit_norm = next_gain is not None
    bm, bn, chunk, vmem = _mm_blocks(m, k, n, 1, 4 + 4 + (2 if emit_norm else 0))
    bn = _MXU_COLS
    blk = pl.BlockSpec((bm, bn), lambda i, j: (i, j))
    in_specs = [_HBM_SPEC, _w_spec(k, bn, layer, 0), blk]
    out_shape = [jax.ShapeDtypeStruct((m, n), F32)]
    out_specs = [blk]
    scratch = _x_scratch(1, bm, k, chunk)
    args = [x, w_stack, res]
    if emit_norm:
        in_specs.append(pl.BlockSpec((1, bn), lambda i, j: (0, j)))
        args.append(next_gain)
        out_shape += [jax.ShapeDtypeStruct((m, n), BF16), jax.ShapeDtypeStruct((m, _LANES), F32)]
        out_specs += [blk, pl.BlockSpec((bm, _LANES), lambda i, j: (i, 0))]
        scratch = scratch + [pltpu.VMEM((bm, _LANES), F32)]
    out = pl.pallas_call(
        functools.partial(_residual_kernel, chunk=chunk, scale=scale, emit_norm=emit_norm, d_total=n),
        out_shape=out_shape,
        grid=(m // bm, n // bn),
        in_specs=in_specs,
        out_specs=out_specs,
        scratch_shapes=scratch,
        compiler_params=_params(_SEQUENTIAL, vmem),
        name="proj_residual",
    )(*args)
    return out if emit_norm else (out[0], None, None)


def _merge_kernel(*refs, chunk):
    nb = N_BRANCH
    br_hbms, g_refs, w_refs, o_ref = refs[0:nb], refs[nb:2 * nb], refs[2 * nb:3 * nb], refs[3 * nb]
    brbufs, sem = refs[3 * nb + 1:4 * nb + 1], refs[4 * nb + 1]

    def compute_chunk(ws, rows):
        acc = None
        for brbuf, g_ref, w in zip(brbufs, g_refs, ws):
            term = g_ref[rows, :].astype(F32) * jnp.dot(brbuf[rows, :], w, preferred_element_type=F32)
            acc = term if acc is None else acc + term
        o_ref[rows, :] = acc.astype(o_ref.dtype)

    _resident_rows(br_hbms, brbufs, sem, chunk,
                   lambda: [w_ref[...].astype(BF16) for w_ref in w_refs], compute_chunk)


def _merge(branches, gates, w_branch, layer, d):
    m, bw = branches[0].shape
    bn = _MXU_COLS
    bm = _largest_divisor(m, _RESIDENT_BYTES // (2 * N_BRANCH * bw), _BF16_ROWS)
    chunk = _largest_divisor(bm, _ACC_ROWS, _BF16_ROWS)
    nb = d // bn
    vmem = (N_BRANCH * bm * bw * 2 + N_BRANCH * 2 * bw * bn * 4 + N_BRANCH * bw * bn * 2
            + N_BRANCH * 2 * bm * bn * 2 + 2 * bm * bn * 2 + 12 * chunk * bn * 4 + 4 * 2**20)
    gate_specs = [pl.BlockSpec((bm, bn), functools.partial(lambda i, j, n: (i, n * nb + j), n=n))
                  for n in range(N_BRANCH)]
    w_specs = [pl.BlockSpec((None, None, bw, bn), functools.partial(lambda i, j, n: (layer, n, 0, j), n=n))
               for n in range(N_BRANCH)]
    return pl.pallas_call(
        functools.partial(_merge_kernel, chunk=chunk),
        out_shape=jax.ShapeDtypeStruct((m, d), BF16),
        grid=(m // bm, nb),
        in_specs=[_HBM_SPEC] * N_BRANCH + gate_specs + w_specs,
        out_specs=pl.BlockSpec((bm, bn), lambda i, j: (i, j)),
        scratch_shapes=_x_scratch(N_BRANCH, bm, bw, chunk),
        compiler_params=_params(_SEQUENTIAL, vmem),
        name="gated_merge",
    )(*branches, *([gates] * N_BRANCH), *([w_branch] * N_BRANCH))


def _layernorm_silu(y, gain, bias):
    mu = jnp.mean(y, axis=-1, keepdims=True)
    c = y - mu
    var = jnp.mean(c * c, axis=-1, keepdims=True)
    return jax.nn.silu(c * lax.rsqrt(var + EPS) * gain + bias)


def _softmax(s):
    e = jnp.exp(s - jnp.max(s, axis=-1, keepdims=True))
    return e / jnp.sum(e, axis=-1, keepdims=True)


_ROW_CHUNK = 64
_LANE_CHUNK = 256


def _prompt_mixer_kernel(ua_ref, bv_ref, bg_ref, ch_ref, cbg_ref, ccg_ref, q_ref, k_ref, v_ref,
                         plin_ref, pscale_ref, cbw_ref, cbb_ref, lng_ref, lnb_ref, ccw_ref,
                         bra_in, brb_in, brc_in, brx_in,
                         bra_ref, brb_ref, brc_ref, brx_ref, npool_ref, ncb_ref, ncc_ref,
                         exta, extb, extc, shifted, conv_scr,
                         *, tile, pad_a, pad_b, pad_c, n_heads):
    del bra_in, brb_in, brc_in, brx_in
    t = pl.program_id(1)
    bw = ua_ref.shape[-1]
    gw = bw // len(POOL_WINDOWS)
    hd = bw // n_heads
    kb = cbw_ref.shape[0]
    kc = ccw_ref.shape[0]
    pool_ctx = npool_ref.shape[0]

    @pl.when(t == 0)
    def _():
        exta[pl.ds(0, pad_a), :] = jnp.zeros((pad_a, bw), F32)
        extb[pl.ds(0, pad_b), :] = jnp.zeros((pad_b, bw), F32)
        extc[pl.ds(0, pad_c), :] = jnp.zeros((pad_c, bw), F32)

    ua = ua_ref[...]
    exta[pl.ds(pad_a, tile), :] = ua
    pos = t * tile + lax.broadcasted_iota(jnp.int32, (tile, 1), 0)
    for g, w in enumerate(POOL_WINDOWS):
        lanes = slice(g * gw, (g + 1) * gw)
        s = ua[:, lanes]
        for i in range(1, w):
            s = s + exta[pl.ds(pad_a - i, tile), lanes]
        cnt = jnp.minimum(float(w), (pos + 1).astype(F32))
        pooled = s / cnt - ua[:, lanes]
        y = jnp.dot(pooled.astype(BF16), plin_ref[g].astype(BF16), preferred_element_type=F32)
        bra_ref[:, lanes] = (y * pscale_ref[:, lanes]).astype(bra_ref.dtype)
    npool_ref[...] = exta[pl.ds(pad_a + tile - pool_ctx, pool_ctx), :]
    exta[pl.ds(0, pad_a), :] = exta[pl.ds(tile, pad_a), :]

    extb[pl.ds(pad_b, tile), :] = bv_ref[...] * jax.nn.sigmoid(bg_ref[...])
    lo = _SUBLANES
    span = pad_b + tile - lo
    for s in range(1, _SUBLANES):
        shifted[s - 1, pl.ds(lo, span), :] = extb[pl.ds(lo - s, span), :]
    for r0 in range(0, tile, _ROW_CHUNK):
        for c0 in range(0, bw, _LANE_CHUNK):
            lanes = slice(c0, c0 + _LANE_CHUNK)
            acc = jnp.broadcast_to(cbb_ref[:, lanes], (_ROW_CHUNK, _LANE_CHUNK))
            for back in range(kb):
                a, s = divmod(back, _SUBLANES)
                rows = pl.ds(pad_b + r0 - _SUBLANES * a, _ROW_CHUNK)
                src = extb[rows, lanes] if s == 0 else shifted[s - 1, rows, lanes]
                acc = acc + cbw_ref[kb - 1 - back:kb - back, lanes] * src
            conv_scr[pl.ds(r0, _ROW_CHUNK), lanes] = acc
    brb_ref[...] = _layernorm_silu(conv_scr[...], lng_ref[...], lnb_ref[...]).astype(brb_ref.dtype)
    ncb_ref[...] = extb[pl.ds(pad_b + tile - (kb - 1), kb - 1), :]
    extb[pl.ds(0, pad_b), :] = extb[pl.ds(tile, pad_b), :]

    extc[pl.ds(pad_c, tile), :] = ccg_ref[...] * ch_ref[...]
    conv = ccw_ref[0:1, :] * extc[pl.ds(pad_c - (kc - 1), tile), :]
    for k in range(1, kc):
        conv = conv + ccw_ref[k:k + 1, :] * extc[pl.ds(pad_c - (kc - 1) + k, tile), :]
    brc_ref[...] = (cbg_ref[...] * conv).astype(brc_ref.dtype)
    ncc_ref[...] = extc[pl.ds(pad_c + tile - (kc - 1), kc - 1), :]
    extc[pl.ds(0, pad_c), :] = extc[pl.ds(tile, pad_c), :]

    q = q_ref[...]
    for h in range(n_heads):
        cols = slice(h * hd, (h + 1) * hd)
        s = lax.dot_general(q[:, cols].astype(BF16), k_ref[:, cols].astype(BF16), (((1,), (1,)), ((), ())),
                            preferred_element_type=F32) * (hd ** -0.5)
        o = jnp.dot(_softmax(s).astype(BF16), v_ref[:, cols].astype(BF16), preferred_element_type=F32)
        brx_ref[:, cols] = o.astype(brx_ref.dtype)


def _prompt_mixers(z, kv, mixp, *, m_total, batch, seq, bw, n_mem, n_heads, pool_ctx):
    plin, pscale, cbw, cbb, lng, lnb, ccw = mixp
    kb, kc = cbw.shape[0], ccw.shape[0]
    tile = _largest_divisor(seq, 256, _ROW_CHUNK)
    nt = seq // tile
    pad_a = _round_up(pool_ctx, _SUBLANES)
    pad_b = _round_up(kb - 1, _SUBLANES) + _SUBLANES
    pad_c = _round_up(kc - 1, _SUBLANES)

    def zspec(col):
        return pl.BlockSpec((tile, bw), lambda b, t: (b * nt + t, col))

    def full(a):
        return pl.BlockSpec(a.shape, lambda b, t: (0,) * a.ndim)

    row_spec = pl.BlockSpec((tile, bw), lambda b, t: (b * nt + t, 0))
    vmem = (2 * 7 * tile * bw * 4 + 2 * 2 * n_mem * bw * 4 + 2 * 4 * tile * bw * 2
            + (pad_a + _SUBLANES * pad_b + pad_c + 11 * tile) * bw * 4 + 16 * tile * bw * 4 + 8 * 2**20)
    kern = functools.partial(_prompt_mixer_kernel, tile=tile, pad_a=pad_a, pad_b=pad_b, pad_c=pad_c,
                             n_heads=n_heads)
    return pl.pallas_call(
        kern,
        out_shape=[jax.ShapeDtypeStruct((m_total, bw), BF16)] * 4 + [
            jax.ShapeDtypeStruct((batch, pool_ctx, bw), F32),
            jax.ShapeDtypeStruct((batch, kb - 1, bw), F32),
            jax.ShapeDtypeStruct((batch, kc - 1, bw), F32)],
        grid=(batch, nt),
        in_specs=[zspec(c) for c in range(7)] + [
            pl.BlockSpec((n_mem, bw), lambda b, t: (b, 0)),
            pl.BlockSpec((n_mem, bw), lambda b, t: (b, 1)),
            full(plin), full(pscale), full(cbw), full(cbb), full(lng), full(lnb), full(ccw)]
            + [_HBM_SPEC] * 4,
        input_output_aliases={16: 0, 17: 1, 18: 2, 19: 3},
        out_specs=[row_spec] * 4 + [
            pl.BlockSpec((None, pool_ctx, bw), lambda b, t: (b, 0, 0)),
            pl.BlockSpec((None, kb - 1, bw), lambda b, t: (b, 0, 0)),
            pl.BlockSpec((None, kc - 1, bw), lambda b, t: (b, 0, 0))],
        scratch_shapes=[pltpu.VMEM((pad_a + tile, bw), F32), pltpu.VMEM((pad_b + tile, bw), F32),
                        pltpu.VMEM((pad_c + tile, bw), F32),
                        pltpu.VMEM((_SUBLANES - 1, pad_b + tile, bw), F32), pltpu.VMEM((tile, bw), F32)],
        compiler_params=_params(("arbitrary", "arbitrary"), vmem),
        name="prompt_mixers",
    )(*([z] * 7), kv, kv, plin, pscale, cbw, cbb, lng, lnb, ccw,
      *[jnp.zeros((m_total, bw), BF16) for _ in range(4)])


def _sample_mixer_kernel(ua_ref, bv_ref, bg_ref, ch_ref, cbg_ref, ccg_ref, pool_ref, cb_ref, cc_ref,
                         plin_ref, pscale_ref, cbw_ref, cbb_ref, lng_ref, lnb_ref, ccw_ref,
                         bra_in, brb_in, brc_in,
                         bra_ref, brb_ref, brc_ref, npool_ref, ncb_ref, ncc_ref):
    del bra_in, brb_in, brc_in
    steps, sb, bw = ua_ref.shape
    gw = bw // len(POOL_WINDOWS)
    kb = cbw_ref.shape[0]
    kc = ccw_ref.shape[0]
    pool_ctx = pool_ref.shape[0]

    def ext_a(j, lanes):
        return pool_ref[j, :, lanes] if j < pool_ctx else ua_ref[j - pool_ctx, :, lanes]

    for l in range(steps):
        for g, w in enumerate(POOL_WINDOWS):
            lanes = slice(g * gw, (g + 1) * gw)
            tok = ext_a(pool_ctx + l, lanes)
            s = tok
            for i in range(1, w):
                s = s + ext_a(pool_ctx + l - i, lanes)
            cnt = float(min(w, PAST_LEN + l + 1))
            pooled = s / cnt - tok
            y = jnp.dot(pooled.astype(BF16), plin_ref[g].astype(BF16), preferred_element_type=F32)
            bra_ref[l, :, lanes] = (y * pscale_ref[:, lanes]).astype(bra_ref.dtype)
    full = slice(0, bw)
    for j in range(pool_ctx):
        npool_ref[j] = ext_a(steps + j, full)

    glu = [bv_ref[l] * jax.nn.sigmoid(bg_ref[l]) for l in range(steps)]

    def ext_b(j):
        return cb_ref[j] if j < kb - 1 else glu[j - (kb - 1)]

    for l in range(steps):
        acc = jnp.broadcast_to(cbb_ref[...], (sb, bw))
        for k in range(kb):
            acc = acc + cbw_ref[k:k + 1, :] * ext_b(l + k)
        brb_ref[l] = _layernorm_silu(acc, lng_ref[...], lnb_ref[...]).astype(brb_ref.dtype)
    for j in range(kb - 1):
        ncb_ref[j] = ext_b(steps + j)

    cin = [ccg_ref[l] * ch_ref[l] for l in range(steps)]

    def ext_c(j):
        return cc_ref[j] if j < kc - 1 else cin[j - (kc - 1)]

    for l in range(steps):
        conv = ccw_ref[0:1, :] * ext_c(l)
        for k in range(1, kc):
            conv = conv + ccw_ref[k:k + 1, :] * ext_c(l + k)
        brc_ref[l] = (cbg_ref[l] * conv).astype(brc_ref.dtype)
    for j in range(kc - 1):
        ncc_ref[j] = ext_c(steps + j)


def _sample_mixers(z3, states_t, mixp, br3, layer, *, steps, n_seq, bw, row_block0):
    pool_t, cb_t, cc_t = states_t
    plin, pscale, cbw, cbb, lng, lnb, ccw = mixp
    sb = _largest_divisor(n_seq, 32, _BF16_ROWS)
    blk0 = row_block0 // steps

    def zspec(col):
        return pl.BlockSpec((steps, sb, bw), lambda c: (blk0, c, col))

    def sspec(a):
        return pl.BlockSpec((None, a.shape[1], sb, bw), lambda c: (layer, 0, c, 0))

    def full(a):
        return pl.BlockSpec(a.shape, lambda c: (0,) * a.ndim)

    any_spec = pl.BlockSpec(memory_space=pl.ANY)
    ctx_rows = pool_t.shape[1] + cb_t.shape[1] + cc_t.shape[1]
    vmem = (2 * 6 * steps * sb * bw * 4 + 4 * ctx_rows * sb * bw * 4 + 2 * 3 * steps * sb * bw * 2
            + 64 * sb * bw * 4 + 8 * 2**20)
    n_in = 16
    return pl.pallas_call(
        _sample_mixer_kernel,
        out_shape=[jax.ShapeDtypeStruct(b.shape, b.dtype) for b in br3] + [
            jax.ShapeDtypeStruct(pool_t.shape[1:], F32),
            jax.ShapeDtypeStruct(cb_t.shape[1:], F32),
            jax.ShapeDtypeStruct(cc_t.shape[1:], F32)],
        grid=(n_seq // sb,),
        in_specs=[zspec(c) for c in range(6)] + [sspec(pool_t), sspec(cb_t), sspec(cc_t),
                  full(plin), full(pscale), full(cbw), full(cbb), full(lng), full(lnb), full(ccw),
                  any_spec, any_spec, any_spec],
        out_specs=[pl.BlockSpec((steps, sb, bw), lambda c: (blk0, c, 0))] * 3 + [
            pl.BlockSpec((pool_t.shape[1], sb, bw), lambda c: (0, c, 0)),
            pl.BlockSpec((cb_t.shape[1], sb, bw), lambda c: (0, c, 0)),
            pl.BlockSpec((cc_t.shape[1], sb, bw), lambda c: (0, c, 0))],
        input_output_aliases={n_in: 0, n_in + 1: 1, n_in + 2: 2},
        compiler_params=_params(("arbitrary",), vmem),
        name="sample_mixers",
    )(*([z3] * 6), pool_t, cb_t, cc_t, plin, pscale, cbw, cbb, lng, lnb, ccw, *br3)


def _sample_attn_kernel(q_ref, k_ref, v_ref, o_ref, *, steps, n_heads, n_mem):
    sb = k_ref.shape[0]
    hd = q_ref.shape[-1] // n_heads
    chunks = hd // _LANES
    rows_per_token = chunks * n_heads
    group = _SUBLANES // steps

    def kv_slice(ref, s, h, c):
        return ref[s, pl.ds(c * n_heads + h, n_mem, stride=rows_per_token), :].astype(BF16)

    scores = []
    for s in range(sb):
        q8 = q_ref[pl.ds((s // group) * _SUBLANES, _SUBLANES), :]
        for h in range(n_heads):
            acc = None
            for c in range(chunks):
                lo = h * hd + c * _LANES
                d = lax.dot_general(q8[:, lo:lo + _LANES].astype(BF16), kv_slice(k_ref, s, h, c),
                                    (((1,), (1,)), ((), ())), preferred_element_type=F32)
                acc = d if acc is None else acc + d
            scores.append(acc)
    p = _softmax(jnp.concatenate(scores, axis=0) * (hd ** -0.5)).astype(BF16)
    row_seq = lax.broadcasted_iota(jnp.int32, (_SUBLANES, _LANES), 0) // steps
    for pair in range(sb // group):
        for h in range(n_heads):
            for c in range(chunks):
                picked = None
                for u in range(group):
                    s = pair * group + u
                    i = s * n_heads + h
                    o = jnp.dot(p[i * _SUBLANES:(i + 1) * _SUBLANES], kv_slice(v_ref, s, h, c),
                                preferred_element_type=F32)
                    picked = o if u == 0 else jnp.where(row_seq == u, o, picked)
                lo = h * hd + c * _LANES
                o_ref[pl.ds(pair * _SUBLANES, _SUBLANES), lo:lo + _LANES] = picked.astype(o_ref.dtype)


def _sample_attention(q, mem_k, mem_v, layer, *, steps, n_heads, n_mem):
    rows, bw = q.shape
    n_seq = rows // steps
    assert _SUBLANES % steps == 0 and (bw // n_heads) % _LANES == 0
    sb = _largest_divisor(n_seq, 8, max(1, _BF16_ROWS // steps))
    kv_spec = pl.BlockSpec((None, sb) + mem_k.shape[2:], lambda c: (layer, c, 0, 0))
    vmem = 2 * 2 * sb * n_mem * bw * 4 + 4 * sb * steps * bw * 4 + 8 * 2**20
    return pl.pallas_call(
        functools.partial(_sample_attn_kernel, steps=steps, n_heads=n_heads, n_mem=n_mem),
        out_shape=jax.ShapeDtypeStruct((rows, bw), BF16),
        grid=(n_seq // sb,),
        in_specs=[pl.BlockSpec((sb * steps, bw), lambda c: (c, 0)), kv_spec, kv_spec],
        out_specs=pl.BlockSpec((sb * steps, bw), lambda c: (c, 0)),
        compiler_params=_params(("parallel",), vmem),
        name="sample_attention",
    )(q, mem_k, mem_v)


def kernel(x_prompt, x_sample, cache_mem_k, cache_mem_v, state_pool, state_conv_b, state_conv_c, mem_prompt, norm_ff1, w_ff1_in, w_ff1_out, norm_mix, w_in, pool_lin, pool_scale, conv_b_w, conv_b_bias, ln_b_gain, ln_b_bias, conv_c_w, norm_mem, w_mem_kv, w_branch, w_o, norm_ff2, w_ff2_in, w_ff2_out, norm_final):
    batch, seq, d = x_prompt.shape
    n_seq, steps, _ = x_sample.shape
    depth = norm_ff1.shape[0]
    bw = d // N_BRANCH
    n_mem, n_heads, hd = cache_mem_k.shape[2:]
    pool_ctx = state_pool.shape[2]
    mp = batch * seq
    ms = n_seq * steps
    m = mp + ms
    n_z = 7 * bw
    chunks = hd // _LANES

    x = jnp.concatenate([x_prompt.reshape(mp, d), x_sample.transpose(1, 0, 2).reshape(ms, d)], axis=0)
    mem = mem_prompt.reshape(batch * n_mem, d)
    states_t = (state_pool.transpose(0, 2, 1, 3), state_conv_b.transpose(0, 2, 1, 3),
                state_conv_c.transpose(0, 2, 1, 3))

    def token_chunk_head_rows(cache):
        c = cache.reshape(depth, n_seq, n_mem, n_heads, chunks, _LANES).transpose(0, 1, 2, 4, 3, 5)
        return c.reshape(depth, n_seq, n_mem * chunks * n_heads, _LANES)

    cache_k = token_chunk_head_rows(cache_mem_k)
    cache_v = token_chunk_head_rows(cache_mem_v)

    mk_p, mv_p, pool_p, cb_p, cc_p, pool_s, cb_s, cc_s = [], [], [], [], [], [], [], []
    h = _rmsnorm(x, norm_ff1[0].reshape(1, d), BF16)
    rinv = None
    for l in range(depth):
        row = lambda a: a[l].reshape(1, -1)
        mixp = (pool_lin[l], row(pool_scale), conv_b_w[l], row(conv_b_bias), row(ln_b_gain), row(ln_b_bias),
                conv_c_w[l])

        x, h, rinv = _matmul_residual(_matmul_swiglu(h, w_ff1_in, l, rinv), w_ff1_out, l, x, 0.5, row(norm_mix))

        z = _matmul(h, w_in, l, 0, n_z, out_dtype=F32, rinv=rinv)
        gates = _matmul(h, w_in, l, n_z, N_BRANCH * d, out_dtype=BF16, rinv=rinv, sigmoid=True)

        kv = _matmul(_rmsnorm(mem, row(norm_mem), BF16), w_mem_kv, l, 0, 2 * bw, out_dtype=F32)
        mk_p.append(kv[:, :bw].reshape(batch, n_mem, n_heads, hd))
        mv_p.append(kv[:, bw:].reshape(batch, n_mem, n_heads, hd))

        bra, brb, brc, brx, npool, ncb, ncc = _prompt_mixers(
            z, kv, mixp, m_total=m, batch=batch, seq=seq, bw=bw, n_mem=n_mem, n_heads=n_heads,
            pool_ctx=pool_ctx)
        pool_p.append(npool); cb_p.append(ncb); cc_p.append(ncc)

        view3 = lambda a: a.reshape(m // n_seq, n_seq, a.shape[-1])
        bra, brb, brc, npool, ncb, ncc = _sample_mixers(
            view3(z), states_t, mixp, [view3(bra), view3(brb), view3(brc)], l,
            steps=steps, n_seq=n_seq, bw=bw, row_block0=mp // n_seq)
        pool_s.append(npool); cb_s.append(ncb); cc_s.append(ncc)

        q_seq = z[mp:, 6 * bw:].reshape(steps, n_seq, bw).transpose(1, 0, 2).reshape(ms, bw)
        o_seq = _sample_attention(q_seq, cache_k, cache_v, l, steps=steps, n_heads=n_heads, n_mem=n_mem)
        o_step = o_seq.reshape(n_seq, steps, bw).transpose(1, 0, 2).reshape(ms, bw)
        brx = lax.dynamic_update_slice(brx, o_step, (mp, 0))

        flat = lambda a: a.reshape(m, bw)
        merged = _merge([flat(bra), flat(brb), flat(brc), brx], gates, w_branch, l, d)
        x, h, rinv = _matmul_residual(merged, w_o, l, x, 1.0, row(norm_ff2))

        next_gain = norm_ff1[l + 1].reshape(1, d) if l + 1 < depth else None
        x, h, rinv = _matmul_residual(_matmul_swiglu(h, w_ff2_in, l, rinv), w_ff2_out, l, x, 0.5, next_gain)

    final_gain = norm_final.reshape(1, d)
    y_prompt = _rmsnorm(x, final_gain, F32, 0, mp).reshape(batch, seq, d)
    y_sample = _rmsnorm(x, final_gain, F32, mp, ms).reshape(steps, n_seq, d).transpose(1, 0, 2)
    unstep = lambda xs: jnp.stack(xs).transpose(0, 2, 1, 3)
    return (y_prompt, y_sample, jnp.stack(mk_p), jnp.stack(mv_p), jnp.stack(pool_p), jnp.stack(cb_p),
            jnp.stack(cc_p), unstep(pool_s), unstep(cb_s), unstep(cc_s))
```

```python
import functools

import jax
import jax.numpy as jnp
from jax import lax
from jax.experimental import pallas as pl
from jax.experimental.pallas import tpu as pltpu

EPS = 1e-6
PAST_LEN = 16384
POOL_WINDOWS = (2, 4, 8, 16)
N_BRANCH = 4

BF16 = jnp.bfloat16
F32 = jnp.float32

_V7X_VMEM_BYTES = 64 * 2**20
_VMEM_CAP = _V7X_VMEM_BYTES - 6 * 2**20
_SUBLANES = 8
_LANES = 128
_BF16_ROWS = 16
_MXU_COLS = 256
_ACC_ROWS = 272
_RESIDENT_BYTES = 18 * 2**20


def _params(semantics, vmem_bytes):
    return pltpu.CompilerParams(dimension_semantics=semantics,
                                vmem_limit_bytes=int(min(_VMEM_CAP, vmem_bytes)))


def _largest_divisor(n, limit, multiple):
    for d in range(min(n, limit), 0, -1):
        if n % d == 0 and d % multiple == 0:
            return d
    raise ValueError(f"no block size for {n} (limit {limit}, multiple {multiple})")


def _round_up(x, m):
    return (x + m - 1) // m * m


def _rmsnorm_kernel(x_ref, g_ref, o_ref):
    x = x_ref[...]
    r = lax.rsqrt(jnp.mean(x * x, axis=-1, keepdims=True) + EPS)
    o_ref[...] = ((x * r) * g_ref[...]).astype(o_ref.dtype)


def _rmsnorm(x, gain, out_dtype, row0=0, rows=None):
    d = x.shape[1]
    rows = x.shape[0] if rows is None else rows
    bm = _largest_divisor(rows, 512, _BF16_ROWS)
    assert row0 % bm == 0
    blk0 = row0 // bm
    osz = jnp.dtype(out_dtype).itemsize
    vmem = 2 * bm * d * (4 + osz) + 4 * bm * d * 4
    return pl.pallas_call(
        _rmsnorm_kernel,
        out_shape=jax.ShapeDtypeStruct((rows, d), out_dtype),
        grid=(rows // bm,),
        in_specs=[pl.BlockSpec((bm, d), lambda i: (blk0 + i, 0)),
                  pl.BlockSpec((1, d), lambda i: (0, 0))],
        out_specs=pl.BlockSpec((bm, d), lambda i: (i, 0)),
        compiler_params=_params(("parallel",), vmem),
        name="rmsnorm",
    )(x, gain)


def _row_scale(acc, rinv):
    n = acc.shape[1]
    return jnp.concatenate([acc[:, c:c + _LANES] * rinv for c in range(0, n, _LANES)], axis=1)


def _resident_rows(x_hbms, xbufs, sem, chunk, prepare, compute_chunk):
    i, j = pl.program_id(0), pl.program_id(1)
    ni, nj = pl.num_programs(0), pl.num_programs(1)
    bm = xbufs[0].shape[0]
    chunks = [pl.ds(r0, chunk) for r0 in range(0, bm, chunk)]

    def copies(blk, c):
        src_rows = pl.ds(blk * bm + c * chunk, chunk)
        return [pltpu.make_async_copy(xh.at[src_rows, :], xb.at[chunks[c], :], sem.at[a, c])
                for a, (xh, xb) in enumerate(zip(x_hbms, xbufs))]

    @pl.when((i == 0) & (j == 0))
    def _():
        for c in range(len(chunks)):
            for cp in copies(0, c):
                cp.start()

    @pl.when(j == 0)
    def _():
        prepared = prepare()
        for c, rows in enumerate(chunks):
            for cp in copies(i, c):
                cp.wait()
            compute_chunk(prepared, rows)

    @pl.when(j > 0)
    def _():
        prepared = prepare()
        for rows in chunks:
            compute_chunk(prepared, rows)

    @pl.when((j == nj - 1) & (i + 1 < ni))
    def _():
        for c in range(len(chunks)):
            for cp in copies(i + 1, c):
                cp.start()


def _proj_kernel(x_ref, w_ref, *rest, chunk, scaled, sigmoid):
    rinv_ref, o_ref = rest if scaled else (None, rest[0])
    w = w_ref[...].astype(BF16)
    for r0 in range(0, x_ref.shape[0], chunk):
        rows = pl.ds(r0, chunk)
        acc = jnp.dot(x_ref[rows, :], w, preferred_element_type=F32)
        if scaled:
            acc = _row_scale(acc, rinv_ref[rows, :])
        if sigmoid:
            acc = jax.nn.sigmoid(acc)
        o_ref[rows, :] = acc.astype(o_ref.dtype)


def _swiglu_kernel(x_hbm, wa_ref, wb_ref, *rest, chunk, scaled):
    (rinv_ref, o_ref, xbuf, sem) = rest if scaled else (None,) + rest

    def compute_chunk(ws, rows):
        x = xbuf[rows, :]
        a = jnp.dot(x, ws[0], preferred_element_type=F32)
        b = jnp.dot(x, ws[1], preferred_element_type=F32)
        if scaled:
            rinv = rinv_ref[rows, :]
            a = _row_scale(a, rinv)
            b = _row_scale(b, rinv)
        o_ref[rows, :] = (jax.nn.silu(a) * b).astype(o_ref.dtype)

    _resident_rows([x_hbm], [xbuf], sem, chunk,
                   lambda: (wa_ref[...].astype(BF16), wb_ref[...].astype(BF16)), compute_chunk)


def _residual_kernel(x_hbm, w_ref, res_ref, *rest, chunk, scale, emit_norm, d_total):
    if emit_norm:
        g_ref, o_ref, og_ref, rinv_ref, xbuf, sem, ssq_ref = rest
        j = pl.program_id(1)

        @pl.when(j == 0)
        def _():
            ssq_ref[...] = jnp.zeros(ssq_ref.shape, F32)
    else:
        o_ref, xbuf, sem = rest
    n = o_ref.shape[1]

    def compute_chunk(w, rows):
        xn = res_ref[rows, :] + scale * jnp.dot(xbuf[rows, :], w, preferred_element_type=F32)
        o_ref[rows, :] = xn
        if emit_norm:
            og_ref[rows, :] = (xn * g_ref[...]).astype(og_ref.dtype)
            sq = xn * xn
            part = sq[:, 0:_LANES]
            for c in range(_LANES, n, _LANES):
                part = part + sq[:, c:c + _LANES]
            ssq_ref[rows, :] += part

    _resident_rows([x_hbm], [xbuf], sem, chunk, lambda: w_ref[...].astype(BF16), compute_chunk)
    if emit_norm:
        @pl.when(j == pl.num_programs(1) - 1)
        def _():
            ms = jnp.sum(ssq_ref[...], axis=-1, keepdims=True) / d_total
            rinv_ref[...] = jnp.broadcast_to(lax.rsqrt(ms + EPS), rinv_ref.shape)


def _mm_blocks(m, k, n, n_weights, out_bytes_per_col):
    bm = _largest_divisor(m, max(_BF16_ROWS, _RESIDENT_BYTES // (2 * k)), _BF16_ROWS)
    chunk = _largest_divisor(bm, _ACC_ROWS, _BF16_ROWS)

    def estimate(bn):
        return (bm * k * 2 + n_weights * (2 * k * bn * 4 + k * bn * 2)
                + 2 * bm * bn * out_bytes_per_col + 2 * bm * _LANES * 4 + 8 * chunk * bn * 4 * n_weights)

    bn = 2 * _MXU_COLS if n % (2 * _MXU_COLS) == 0 and estimate(2 * _MXU_COLS) <= 54 * 2**20 else _MXU_COLS
    return bm, bn, chunk, estimate(bn) + 4 * 2**20


_HBM_SPEC = pl.BlockSpec(memory_space=pl.ANY)
_SEQUENTIAL = ("arbitrary", "arbitrary")


def _x_scratch(n_arrays, bm, k, chunk):
    return [pltpu.VMEM((bm, k), BF16)] * n_arrays + [pltpu.SemaphoreType.DMA((n_arrays, bm // chunk))]


def _w_spec(k, bn, layer, block0):
    return pl.BlockSpec((None, k, bn), lambda i, j: (layer, 0, block0 + j))


def _rinv_spec(bm):
    return pl.BlockSpec((bm, _LANES), lambda i, j: (i, 0), pipeline_mode=pl.Buffered(1))


def _matmul(x, w_stack, layer, col0, n, *, out_dtype, rinv=None, sigmoid=False):
    m, k = x.shape
    bm, bn, chunk, vmem = _mm_blocks(m, k, n, 1, jnp.dtype(out_dtype).itemsize)
    assert col0 % bn == 0
    scaled = rinv is not None
    return pl.pallas_call(
        functools.partial(_proj_kernel, chunk=chunk, scaled=scaled, sigmoid=sigmoid),
        out_shape=jax.ShapeDtypeStruct((m, n), out_dtype),
        grid=(m // bm, n // bn),
        in_specs=[pl.BlockSpec((bm, k), lambda i, j: (i, 0), pipeline_mode=pl.Buffered(1)),
                  _w_spec(k, bn, layer, col0 // bn)] + ([_rinv_spec(bm)] if scaled else []),
        out_specs=pl.BlockSpec((bm, bn), lambda i, j: (i, j)),
        compiler_params=_params(("parallel", "arbitrary"), vmem),
        name="proj_sigmoid" if sigmoid else "proj",
    )(x, w_stack, *([rinv] if scaled else []))


def _matmul_swiglu(x, w_stack, layer, rinv=None):
    m, k = x.shape
    f = w_stack.shape[-1] // 2
    bm, bn, chunk, vmem = _mm_blocks(m, k, f, 2, 2)
    chunk = _largest_divisor(bm, 2 * _ACC_ROWS, _BF16_ROWS)
    scaled = rinv is not None
    return pl.pallas_call(
        functools.partial(_swiglu_kernel, chunk=chunk, scaled=scaled),
        out_shape=jax.ShapeDtypeStruct((m, f), BF16),
        grid=(m // bm, f // bn),
        in_specs=[_HBM_SPEC, _w_spec(k, bn, layer, 0), _w_spec(k, bn, layer, f // bn)]
                 + ([_rinv_spec(bm)] if scaled else []),
        out_specs=pl.BlockSpec((bm, bn), lambda i, j: (i, j)),
        scratch_shapes=_x_scratch(1, bm, k, chunk),
        compiler_params=_params(_SEQUENTIAL, vmem),
        name="ffn_in_swiglu",
    )(x, w_stack, w_stack, *([rinv] if scaled else []))


def _matmul_residual(x, w_stack, layer, res, scale, next_gain=None):
    m, k = x.shape
    n = w_stack.shape[-1]
    emit_norm = next_gain is not None
    bm, bn, chunk, vmem = _mm_blocks(m, k, n, 1, 4 + 4 + (2 if emit_norm else 0))
    bn = _MXU_COLS
    blk = pl.BlockSpec((bm, bn), lambda i, j: (i, j))
    in_specs = [_HBM_SPEC, _w_spec(k, bn, layer, 0), blk]
    out_shape = [jax.ShapeDtypeStruct((m, n), F32)]
    out_specs = [blk]
    scratch = _x_scratch(1, bm, k, chunk)
    args = [x, w_stack, res]
    if emit_norm:
        in_specs.append(pl.BlockSpec((1, bn), lambda i, j: (0, j)))
        args.append(next_gain)
        out_shape += [jax.ShapeDtypeStruct((m, n), BF16), jax.ShapeDtypeStruct((m, _LANES), F32)]
        out_specs += [blk, pl.BlockSpec((bm, _LANES), lambda i, j: (i, 0))]
        scratch = scratch + [pltpu.VMEM((bm, _LANES), F32)]
    out = pl.pallas_call(
        functools.partial(_residual_kernel, chunk=chunk, scale=scale, emit_norm=emit_norm, d_total=n),
        out_shape=out_shape,
        grid=(m // bm, n // bn),
        in_specs=in_specs,
        out_specs=out_specs,
        scratch_shapes=scratch,
        compiler_params=_params(_SEQUENTIAL, vmem),
        name="proj_residual",
    )(*args)
    return out if emit_norm else (out[0], None, None)


def _merge_kernel(*refs, chunk):
    nb = N_BRANCH
    br_hbms, g_refs, w_refs, o_ref = refs[0:nb], refs[nb:2 * nb], refs[2 * nb:3 * nb], refs[3 * nb]
    brbufs, sem = refs[3 * nb + 1:4 * nb + 1], refs[4 * nb + 1]

    def compute_chunk(ws, rows):
        acc = None
        for brbuf, g_ref, w in zip(brbufs, g_refs, ws):
            term = g_ref[rows, :].astype(F32) * jnp.dot(brbuf[rows, :], w, preferred_element_type=F32)
            acc = term if acc is None else acc + term
        o_ref[rows, :] = acc.astype(o_ref.dtype)

    _resident_rows(br_hbms, brbufs, sem, chunk,
                   lambda: [w_ref[...].astype(BF16) for w_ref in w_refs], compute_chunk)


def _merge(branches, gates, w_branch, layer, d):
    m, bw = branches[0].shape
    bn = _MXU_COLS
    bm = _largest_divisor(m, _RESIDENT_BYTES // (2 * N_BRANCH * bw), _BF16_ROWS)
    chunk = _largest_divisor(bm, _ACC_ROWS, _BF16_ROWS)
    nb = d // bn
    vmem = (N_BRANCH * bm * bw * 2 + N_BRANCH * 2 * bw * bn * 4 + N_BRANCH * bw * bn * 2
            + N_BRANCH * 2 * bm * bn * 2 + 2 * bm * bn * 2 + 12 * chunk * bn * 4 + 4 * 2**20)
    gate_specs = [pl.BlockSpec((bm, bn), functools.partial(lambda i, j, n: (i, n * nb + j), n=n))
                  for n in range(N_BRANCH)]
    w_specs = [pl.BlockSpec((None, None, bw, bn), functools.partial(lambda i, j, n: (layer, n, 0, j), n=n))
               for n in range(N_BRANCH)]
    return pl.pallas_call(
        functools.partial(_merge_kernel, chunk=chunk),
        out_shape=jax.ShapeDtypeStruct((m, d), BF16),
        grid=(m // bm, nb),
        in_specs=[_HBM_SPEC] * N_BRANCH + gate_specs + w_specs,
        out_specs=pl.BlockSpec((bm, bn), lambda i, j: (i, j)),
        scratch_shapes=_x_scratch(N_BRANCH, bm, bw, chunk),
        compiler_params=_params(_SEQUENTIAL, vmem),
        name="gated_merge",
    )(*branches, *([gates] * N_BRANCH), *([w_branch] * N_BRANCH))


def _layernorm_silu(y, gain, bias):
    mu = jnp.mean(y, axis=-1, keepdims=True)
    c = y - mu
    var = jnp.mean(c * c, axis=-1, keepdims=True)
    return jax.nn.silu(c * lax.rsqrt(var + EPS) * gain + bias)


def _softmax(s):
    e = jnp.exp(s - jnp.max(s, axis=-1, keepdims=True))
    return e / jnp.sum(e, axis=-1, keepdims=True)


_ROW_CHUNK = 64
_LANE_CHUNK = 256


_N_MIXER_INPUTS = 16


def _prompt_mixer_kernel(*refs, batch, **static):
    b = pl.program_id(0)

    @pl.when(b < batch)
    def _():
        _prompt_mixer_body(*refs, **static)

    @pl.when(b == batch)
    def _():
        for o_ref in refs[_N_MIXER_INPUTS:_N_MIXER_INPUTS + N_BRANCH]:
            o_ref[...] = jnp.zeros(o_ref.shape, o_ref.dtype)


def _prompt_mixer_body(ua_ref, bv_ref, bg_ref, ch_ref, cbg_ref, ccg_ref, q_ref, k_ref, v_ref,
                       plin_ref, pscale_ref, cbw_ref, cbb_ref, lng_ref, lnb_ref, ccw_ref,
                       bra_ref, brb_ref, brc_ref, brx_ref, npool_ref, ncb_ref, ncc_ref,
                       exta, extb, extc, shifted, conv_scr,
                       *, tile, pad_a, pad_b, pad_c, n_heads):
    t = pl.program_id(1)
    bw = ua_ref.shape[-1]
    gw = bw // len(POOL_WINDOWS)
    hd = bw // n_heads
    kb = cbw_ref.shape[0]
    kc = ccw_ref.shape[0]
    pool_ctx = npool_ref.shape[0]

    @pl.when(t == 0)
    def _():
        exta[pl.ds(0, pad_a), :] = jnp.zeros((pad_a, bw), F32)
        extb[pl.ds(0, pad_b), :] = jnp.zeros((pad_b, bw), F32)
        extc[pl.ds(0, pad_c), :] = jnp.zeros((pad_c, bw), F32)

    ua = ua_ref[...]
    exta[pl.ds(pad_a, tile), :] = ua
    pos = t * tile + lax.broadcasted_iota(jnp.int32, (tile, 1), 0)
    for g, w in enumerate(POOL_WINDOWS):
        lanes = slice(g * gw, (g + 1) * gw)
        s = ua[:, lanes]
        for i in range(1, w):
            s = s + exta[pl.ds(pad_a - i, tile), lanes]
        cnt = jnp.minimum(float(w), (pos + 1).astype(F32))
        pooled = s / cnt - ua[:, lanes]
        y = jnp.dot(pooled.astype(BF16), plin_ref[g].astype(BF16), preferred_element_type=F32)
        bra_ref[:, lanes] = (y * pscale_ref[:, lanes]).astype(bra_ref.dtype)
    npool_ref[...] = exta[pl.ds(pad_a + tile - pool_ctx, pool_ctx), :]
    exta[pl.ds(0, pad_a), :] = exta[pl.ds(tile, pad_a), :]

    extb[pl.ds(pad_b, tile), :] = bv_ref[...] * jax.nn.sigmoid(bg_ref[...])
    lo = _SUBLANES
    span = pad_b + tile - lo
    for s in range(1, _SUBLANES):
        shifted[s - 1, pl.ds(lo, span), :] = extb[pl.ds(lo - s, span), :]
    for r0 in range(0, tile, _ROW_CHUNK):
        for c0 in range(0, bw, _LANE_CHUNK):
            lanes = slice(c0, c0 + _LANE_CHUNK)
            acc = jnp.broadcast_to(cbb_ref[:, lanes], (_ROW_CHUNK, _LANE_CHUNK))
            for back in range(kb):
                a, s = divmod(back, _SUBLANES)
                rows = pl.ds(pad_b + r0 - _SUBLANES * a, _ROW_CHUNK)
                src = extb[rows, lanes] if s == 0 else shifted[s - 1, rows, lanes]
                acc = acc + cbw_ref[kb - 1 - back:kb - back, lanes] * src
            conv_scr[pl.ds(r0, _ROW_CHUNK), lanes] = acc
    brb_ref[...] = _layernorm_silu(conv_scr[...], lng_ref[...], lnb_ref[...]).astype(brb_ref.dtype)
    ncb_ref[...] = extb[pl.ds(pad_b + tile - (kb - 1), kb - 1), :]
    extb[pl.ds(0, pad_b), :] = extb[pl.ds(tile, pad_b), :]

    extc[pl.ds(pad_c, tile), :] = ccg_ref[...] * ch_ref[...]
    conv = ccw_ref[0:1, :] * extc[pl.ds(pad_c - (kc - 1), tile), :]
    for k in range(1, kc):
        conv = conv + ccw_ref[k:k + 1, :] * extc[pl.ds(pad_c - (kc - 1) + k, tile), :]
    brc_ref[...] = (cbg_ref[...] * conv).astype(brc_ref.dtype)
    ncc_ref[...] = extc[pl.ds(pad_c + tile - (kc - 1), kc - 1), :]
    extc[pl.ds(0, pad_c), :] = extc[pl.ds(tile, pad_c), :]

    q = q_ref[...]
    for h in range(n_heads):
        cols = slice(h * hd, (h + 1) * hd)
        s = lax.dot_general(q[:, cols].astype(BF16), k_ref[:, cols].astype(BF16), (((1,), (1,)), ((), ())),
                            preferred_element_type=F32) * (hd ** -0.5)
        o = jnp.dot(_softmax(s).astype(BF16), v_ref[:, cols].astype(BF16), preferred_element_type=F32)
        brx_ref[:, cols] = o.astype(brx_ref.dtype)


def _prompt_mixers(z, kv, mixp, *, m_total, batch, seq, bw, n_mem, n_heads, pool_ctx):
    plin, pscale, cbw, cbb, lng, lnb, ccw = mixp
    kb, kc = cbw.shape[0], ccw.shape[0]
    tile = _largest_divisor(seq, 256, _ROW_CHUNK)
    nt = seq // tile
    pad_a = _round_up(pool_ctx, _SUBLANES)
    pad_b = _round_up(kb - 1, _SUBLANES) + _SUBLANES
    pad_c = _round_up(kc - 1, _SUBLANES)

    n_blocks = pl.cdiv(m_total, tile)
    assert batch * nt + nt >= n_blocks

    def row_block(b, t):
        return jnp.minimum(b * nt + t, n_blocks - 1)

    def seq_index(b):
        return jnp.minimum(b, batch - 1)

    def zspec(col):
        return pl.BlockSpec((tile, bw), lambda b, t: (row_block(b, t), col))

    def full(a):
        return pl.BlockSpec(a.shape, lambda b, t: (0,) * a.ndim)

    row_spec = pl.BlockSpec((tile, bw), lambda b, t: (row_block(b, t), 0))
    vmem = (2 * 7 * tile * bw * 4 + 2 * 2 * n_mem * bw * 4 + 2 * 4 * tile * bw * 2
            + (pad_a + _SUBLANES * pad_b + pad_c + 11 * tile) * bw * 4 + 16 * tile * bw * 4 + 8 * 2**20)
    kern = functools.partial(_prompt_mixer_kernel, batch=batch, tile=tile, pad_a=pad_a, pad_b=pad_b,
                             pad_c=pad_c, n_heads=n_heads)
    assert len(mixp) + 9 == _N_MIXER_INPUTS
    return pl.pallas_call(
        kern,
        out_shape=[jax.ShapeDtypeStruct((m_total, bw), BF16)] * 4 + [
            jax.ShapeDtypeStruct((batch, pool_ctx, bw), F32),
            jax.ShapeDtypeStruct((batch, kb - 1, bw), F32),
            jax.ShapeDtypeStruct((batch, kc - 1, bw), F32)],
        grid=(batch + 1, nt),
        in_specs=[zspec(c) for c in range(7)] + [
            pl.BlockSpec((n_mem, bw), lambda b, t: (seq_index(b), 0)),
            pl.BlockSpec((n_mem, bw), lambda b, t: (seq_index(b), 1)),
            full(plin), full(pscale), full(cbw), full(cbb), full(lng), full(lnb), full(ccw)],
        out_specs=[row_spec] * 4 + [
            pl.BlockSpec((None, pool_ctx, bw), lambda b, t: (seq_index(b), 0, 0)),
            pl.BlockSpec((None, kb - 1, bw), lambda b, t: (seq_index(b), 0, 0)),
            pl.BlockSpec((None, kc - 1, bw), lambda b, t: (seq_index(b), 0, 0))],
        scratch_shapes=[pltpu.VMEM((pad_a + tile, bw), F32), pltpu.VMEM((pad_b + tile, bw), F32),
                        pltpu.VMEM((pad_c + tile, bw), F32),
                        pltpu.VMEM((_SUBLANES - 1, pad_b + tile, bw), F32), pltpu.VMEM((tile, bw), F32)],
        compiler_params=_params(("arbitrary", "arbitrary"), vmem),
        name="prompt_mixers",
    )(*([z] * 7), kv, kv, plin, pscale, cbw, cbb, lng, lnb, ccw)


def _sample_mixer_kernel(ua_ref, bv_ref, bg_ref, ch_ref, cbg_ref, ccg_ref, pool_ref, cb_ref, cc_ref,
                         plin_ref, pscale_ref, cbw_ref, cbb_ref, lng_ref, lnb_ref, ccw_ref,
                         bra_in, brb_in, brc_in,
                         bra_ref, brb_ref, brc_ref, npool_ref, ncb_ref, ncc_ref):
    del bra_in, brb_in, brc_in
    steps, sb, bw = ua_ref.shape
    gw = bw // len(POOL_WINDOWS)
    kb = cbw_ref.shape[0]
    kc = ccw_ref.shape[0]
    pool_ctx = pool_ref.shape[0]

    def ext_a(j, lanes):
        return pool_ref[j, :, lanes] if j < pool_ctx else ua_ref[j - pool_ctx, :, lanes]

    for l in range(steps):
        for g, w in enumerate(POOL_WINDOWS):
            lanes = slice(g * gw, (g + 1) * gw)
            tok = ext_a(pool_ctx + l, lanes)
            s = tok
            for i in range(1, w):
                s = s + ext_a(pool_ctx + l - i, lanes)
            cnt = float(min(w, PAST_LEN + l + 1))
            pooled = s / cnt - tok
            y = jnp.dot(pooled.astype(BF16), plin_ref[g].astype(BF16), preferred_element_type=F32)
            bra_ref[l, :, lanes] = (y * pscale_ref[:, lanes]).astype(bra_ref.dtype)
    full = slice(0, bw)
    for j in range(pool_ctx):
        npool_ref[j] = ext_a(steps + j, full)

    glu = [bv_ref[l] * jax.nn.sigmoid(bg_ref[l]) for l in range(steps)]

    def ext_b(j):
        return cb_ref[j] if j < kb - 1 else glu[j - (kb - 1)]

    for l in range(steps):
        acc = jnp.broadcast_to(cbb_ref[...], (sb, bw))
        for k in range(kb):
            acc = acc + cbw_ref[k:k + 1, :] * ext_b(l + k)
        brb_ref[l] = _layernorm_silu(acc, lng_ref[...], lnb_ref[...]).astype(brb_ref.dtype)
    for j in range(kb - 1):
        ncb_ref[j] = ext_b(steps + j)

    cin = [ccg_ref[l] * ch_ref[l] for l in range(steps)]

    def ext_c(j):
        return cc_ref[j] if j < kc - 1 else cin[j - (kc - 1)]

    for l in range(steps):
        conv = ccw_ref[0:1, :] * ext_c(l)
        for k in range(1, kc):
            conv = conv + ccw_ref[k:k + 1, :] * ext_c(l + k)
        brc_ref[l] = (cbg_ref[l] * conv).astype(brc_ref.dtype)
    for j in range(kc - 1):
        ncc_ref[j] = ext_c(steps + j)


def _sample_mixers(z3, states_t, mixp, br3, layer, *, steps, n_seq, bw, row_block0):
    pool_t, cb_t, cc_t = states_t
    plin, pscale, cbw, cbb, lng, lnb, ccw = mixp
    sb = _largest_divisor(n_seq, 32, _BF16_ROWS)
    blk0 = row_block0 // steps

    def zspec(col):
        return pl.BlockSpec((steps, sb, bw), lambda c: (blk0, c, col))

    def sspec(a):
        return pl.BlockSpec((None, a.shape[1], sb, bw), lambda c: (layer, 0, c, 0))

    def full(a):
        return pl.BlockSpec(a.shape, lambda c: (0,) * a.ndim)

    any_spec = pl.BlockSpec(memory_space=pl.ANY)
    ctx_rows = pool_t.shape[1] + cb_t.shape[1] + cc_t.shape[1]
    vmem = (2 * 6 * steps * sb * bw * 4 + 4 * ctx_rows * sb * bw * 4 + 2 * 3 * steps * sb * bw * 2
            + 64 * sb * bw * 4 + 8 * 2**20)
    n_in = 16
    return pl.pallas_call(
        _sample_mixer_kernel,
        out_shape=[jax.ShapeDtypeStruct(b.shape, b.dtype) for b in br3] + [
            jax.ShapeDtypeStruct(pool_t.shape[1:], F32),
            jax.ShapeDtypeStruct(cb_t.shape[1:], F32),
            jax.ShapeDtypeStruct(cc_t.shape[1:], F32)],
        grid=(n_seq // sb,),
        in_specs=[zspec(c) for c in range(6)] + [sspec(pool_t), sspec(cb_t), sspec(cc_t),
                  full(plin), full(pscale), full(cbw), full(cbb), full(lng), full(lnb), full(ccw),
                  any_spec, any_spec, any_spec],
        out_specs=[pl.BlockSpec((steps, sb, bw), lambda c: (blk0, c, 0))] * 3 + [
            pl.BlockSpec((pool_t.shape[1], sb, bw), lambda c: (0, c, 0)),
            pl.BlockSpec((cb_t.shape[1], sb, bw), lambda c: (0, c, 0)),
            pl.BlockSpec((cc_t.shape[1], sb, bw), lambda c: (0, c, 0))],
        input_output_aliases={n_in: 0, n_in + 1: 1, n_in + 2: 2},
        compiler_params=_params(("arbitrary",), vmem),
        name="sample_mixers",
    )(*([z3] * 6), pool_t, cb_t, cc_t, plin, pscale, cbw, cbb, lng, lnb, ccw, *br3)


def _sample_attn_kernel(q_ref, k_ref, v_ref, o_ref, *, steps, n_heads, n_mem):
    sb = k_ref.shape[0]
    hd = q_ref.shape[-1] // n_heads
    chunks = hd // _LANES
    rows_per_token = chunks * n_heads
    group = _SUBLANES // steps

    def kv_slice(ref, s, h, c):
        return ref[s, pl.ds(c * n_heads + h, n_mem, stride=rows_per_token), :].astype(BF16)

    scores = []
    for s in range(sb):
        q8 = q_ref[pl.ds((s // group) * _SUBLANES, _SUBLANES), :]
        for h in range(n_heads):
            acc = None
            for c in range(chunks):
                lo = h * hd + c * _LANES
                d = lax.dot_general(q8[:, lo:lo + _LANES].astype(BF16), kv_slice(k_ref, s, h, c),
                                    (((1,), (1,)), ((), ())), preferred_element_type=F32)
                acc = d if acc is None else acc + d
            scores.append(acc)
    p = _softmax(jnp.concatenate(scores, axis=0) * (hd ** -0.5)).astype(BF16)
    row_seq = lax.broadcasted_iota(jnp.int32, (_SUBLANES, _LANES), 0) // steps
    for pair in range(sb // group):
        for h in range(n_heads):
            for c in range(chunks):
                picked = None
                for u in range(group):
                    s = pair * group + u
                    i = s * n_heads + h
                    o = jnp.dot(p[i * _SUBLANES:(i + 1) * _SUBLANES], kv_slice(v_ref, s, h, c),
                                preferred_element_type=F32)
                    picked = o if u == 0 else jnp.where(row_seq == u, o, picked)
                lo = h * hd + c * _LANES
                o_ref[pl.ds(pair * _SUBLANES, _SUBLANES), lo:lo + _LANES] = picked.astype(o_ref.dtype)


def _sample_attention(q, mem_k, mem_v, layer, *, steps, n_heads, n_mem):
    rows, bw = q.shape
    n_seq = rows // steps
    assert _SUBLANES % steps == 0 and (bw // n_heads) % _LANES == 0
    sb = _largest_divisor(n_seq, 8, max(1, _BF16_ROWS // steps))
    kv_spec = pl.BlockSpec((None, sb) + mem_k.shape[2:], lambda c: (layer, c, 0, 0))
    vmem = 2 * 2 * sb * n_mem * bw * 4 + 4 * sb * steps * bw * 4 + 8 * 2**20
    return pl.pallas_call(
        functools.partial(_sample_attn_kernel, steps=steps, n_heads=n_heads, n_mem=n_mem),
        out_shape=jax.ShapeDtypeStruct((rows, bw), BF16),
        grid=(n_seq // sb,),
        in_specs=[pl.BlockSpec((sb * steps, bw), lambda c: (c, 0)), kv_spec, kv_spec],
        out_specs=pl.BlockSpec((sb * steps, bw), lambda c: (c, 0)),
        compiler_params=_params(("parallel",), vmem),
        name="sample_attention",
    )(q, mem_k, mem_v)


def kernel(x_prompt, x_sample, cache_mem_k, cache_mem_v, state_pool, state_conv_b, state_conv_c, mem_prompt, norm_ff1, w_ff1_in, w_ff1_out, norm_mix, w_in, pool_lin, pool_scale, conv_b_w, conv_b_bias, ln_b_gain, ln_b_bias, conv_c_w, norm_mem, w_mem_kv, w_branch, w_o, norm_ff2, w_ff2_in, w_ff2_out, norm_final):
    batch, seq, d = x_prompt.shape
    n_seq, steps, _ = x_sample.shape
    depth = norm_ff1.shape[0]
    bw = d // N_BRANCH
    n_mem, n_heads, hd = cache_mem_k.shape[2:]
    pool_ctx = state_pool.shape[2]
    mp = batch * seq
    ms = n_seq * steps
    m = mp + ms
    n_z = 7 * bw
    chunks = hd // _LANES

    x = jnp.concatenate([x_prompt.reshape(mp, d), x_sample.transpose(1, 0, 2).reshape(ms, d)], axis=0)
    mem = mem_prompt.reshape(batch * n_mem, d)
    states_t = (state_pool.transpose(0, 2, 1, 3), state_conv_b.transpose(0, 2, 1, 3),
                state_conv_c.transpose(0, 2, 1, 3))

    def token_chunk_head_rows(cache):
        c = cache.reshape(depth, n_seq, n_mem, n_heads, chunks, _LANES).transpose(0, 1, 2, 4, 3, 5)
        return c.reshape(depth, n_seq, n_mem * chunks * n_heads, _LANES)

    cache_k = token_chunk_head_rows(cache_mem_k)
    cache_v = token_chunk_head_rows(cache_mem_v)

    mk_p, mv_p, pool_p, cb_p, cc_p, pool_s, cb_s, cc_s = [], [], [], [], [], [], [], []
    h = _rmsnorm(x, norm_ff1[0].reshape(1, d), BF16)
    rinv = None
    for l in range(depth):
        row = lambda a: a[l].reshape(1, -1)
        mixp = (pool_lin[l], row(pool_scale), conv_b_w[l], row(conv_b_bias), row(ln_b_gain), row(ln_b_bias),
                conv_c_w[l])

        x, h, rinv = _matmul_residual(_matmul_swiglu(h, w_ff1_in, l, rinv), w_ff1_out, l, x, 0.5, row(norm_mix))

        z = _matmul(h, w_in, l, 0, n_z, out_dtype=F32, rinv=rinv)
        gates = _matmul(h, w_in, l, n_z, N_BRANCH * d, out_dtype=BF16, rinv=rinv, sigmoid=True)

        kv = _matmul(_rmsnorm(mem, row(norm_mem), BF16), w_mem_kv, l, 0, 2 * bw, out_dtype=F32)
        mk_p.append(kv[:, :bw].reshape(batch, n_mem, n_heads, hd))
        mv_p.append(kv[:, bw:].reshape(batch, n_mem, n_heads, hd))

        bra, brb, brc, brx, npool, ncb, ncc = _prompt_mixers(
            z, kv, mixp, m_total=m, batch=batch, seq=seq, bw=bw, n_mem=n_mem, n_heads=n_heads,
            pool_ctx=pool_ctx)
        pool_p.append(npool); cb_p.append(ncb); cc_p.append(ncc)

        view3 = lambda a: a.reshape(m // n_seq, n_seq, a.shape[-1])
        bra, brb, brc, npool, ncb, ncc = _sample_mixers(
            view3(z), states_t, mixp, [view3(bra), view3(brb), view3(brc)], l,
            steps=steps, n_seq=n_seq, bw=bw, row_block0=mp // n_seq)
        pool_s.append(npool); cb_s.append(ncb); cc_s.append(ncc)

        q_seq = z[mp:, 6 * bw:].reshape(steps, n_seq, bw).transpose(1, 0, 2).reshape(ms, bw)
        o_seq = _sample_attention(q_seq, cache_k, cache_v, l, steps=steps, n_heads=n_heads, n_mem=n_mem)
        o_step = o_seq.reshape(n_seq, steps, bw).transpose(1, 0, 2).reshape(ms, bw)
        brx = lax.dynamic_update_slice(brx, o_step, (mp, 0))

        flat = lambda a: a.reshape(m, bw)
        merged = _merge([flat(bra), flat(brb), flat(brc), brx], gates, w_branch, l, d)
        x, h, rinv = _matmul_residual(merged, w_o, l, x, 1.0, row(norm_ff2))

        next_gain = norm_ff1[l + 1].reshape(1, d) if l + 1 < depth else None
        x, h, rinv = _matmul_residual(_matmul_swiglu(h, w_ff2_in, l, rinv), w_ff2_out, l, x, 0.5, next_gain)

    final_gain = norm_final.reshape(1, d)
    y_prompt = _rmsnorm(x, final_gain, F32, 0, mp).reshape(batch, seq, d)
    y_sample = _rmsnorm(x, final_gain, F32, mp, ms).reshape(steps, n_seq, d).transpose(1, 0, 2)
    unstep = lambda xs: jnp.stack(xs).transpose(0, 2, 1, 3)
    return (y_prompt, y_sample, jnp.stack(mk_p), jnp.stack(mv_p), jnp.stack(pool_p), jnp.stack(cb_p),
            jnp.stack(cc_p), unstep(pool_s), unstep(cb_s), unstep(cc_s))
```
